```python
import jax, jax.numpy as jnp
from jax import lax
import numpy as np

D_MODEL = 1024
BATCH = 4
SEQ = 4096
DEPTH = 4
DEC_BATCH = 128
DEC_SEQ = 1
PAST_LEN = 2048
PAGE_SIZE = 128

N_MIXERS = 3
N_HEADS = 16
HEAD_DIM = D_MODEL // N_HEADS
Q_BLOCK = 128
SB_BIAS_NEAR = -3.0
SB_BIAS_FAR = -9.0
POOL_WINDOWS = (2, 4, 8, 16)
N_POOL_GROUPS = len(POOL_WINDOWS)
POOL_GROUP = D_MODEL // N_POOL_GROUPS
POOL_BUF = max(POOL_WINDOWS) - 1
D_RNN = D_MODEL
RG_HEADS = 4
RG_BLOCK = D_RNN // RG_HEADS
CONV_WIDTH = 4
RG_C = 8.0
D_FF = 4 * D_MODEL
N_META = 16
RMS_EPS = 1e-6

N_ATTN_LAYERS = len(range(0, DEPTH, N_MIXERS))
N_POOLMIX_LAYERS = len(range(1, DEPTH, N_MIXERS))
N_RGLRU_LAYERS = len(range(2, DEPTH, N_MIXERS))

kernel_name = 'hybrid_stickbreak_pool_rglru_decoder_step'


def rms_norm(x, g):
    xf = x.astype(jnp.float32)
    y = xf * lax.rsqrt(jnp.mean(xf * xf, axis=-1, keepdims=True) + RMS_EPS)
    return (y * g.astype(jnp.float32)).astype(x.dtype)


def stick_breaking_attention(q, k, v, bias, q_start):
    tq, tk = q.shape[1], k.shape[1]
    scale = HEAD_DIM ** -0.5
    bf = bias.astype(jnp.float32)[None, :, None, None]
    outs = []
    for b0 in range(0, tq, Q_BLOCK):
        b1 = min(b0 + Q_BLOCK, tq)
        n_keys = max(1, min(tk, q_start + b1 - 1))
        qb = q[:, b0:b1].astype(jnp.float32)
        kb = k[:, :n_keys].astype(jnp.float32)
        vb = v[:, :n_keys].astype(jnp.float32)
        z = jnp.einsum('bqhd,bkhd->bhqk', qb, kb) * scale + bf
        t_pos = q_start + jnp.arange(b0, b1)
        s_pos = jnp.arange(n_keys)
        mask = (s_pos[None, :] < t_pos[:, None])[None, None]
        log_keep = jnp.where(mask, jax.nn.log_sigmoid(-z), 0.0)
        after = lax.cumsum(log_keep, axis=3, reverse=True) - log_keep
        w = jnp.where(mask, jnp.exp(jax.nn.log_sigmoid(z) + after), 0.0)
        outs.append(jnp.einsum('bhqk,bkhd->bqhd', w, vb))
    return jnp.concatenate(outs, axis=1).astype(q.dtype)


def attn_mixer(u, w_qkv, w_o, bias, k_past, v_past, q_start):
    B, T, _ = u.shape
    qkv = (u @ w_qkv).reshape(B, T, 3, N_HEADS, HEAD_DIM)
    q, k, v = qkv[:, :, 0], qkv[:, :, 1], qkv[:, :, 2]
    if k_past is None:
        k_all, v_all = k, v
    else:
        k_all = jnp.concatenate([k_past.astype(k.dtype), k], axis=1)
        v_all = jnp.concatenate([v_past.astype(v.dtype), v], axis=1)
    o = stick_breaking_attention(q, k_all, v_all, bias, q_start)
    return o.reshape(B, T, D_MODEL) @ w_o, k, v


def pool_mixer(u, buf, q_start, w_pool, pool_scale):
    B, T, _ = u.shape
    u_ext = u if buf is None else jnp.concatenate([buf.astype(u.dtype), u], axis=1)
    L = u_ext.shape[1]
    uf = u_ext.astype(jnp.float32)
    csum = jnp.cumsum(uf, axis=1)
    pos = q_start + jnp.arange(T)
    groups = []
    for gi, w in enumerate(POOL_WINDOWS):
        sl = slice(gi * POOL_GROUP, (gi + 1) * POOL_GROUP)
        cp = jnp.concatenate([jnp.zeros((B, w, POOL_GROUP), jnp.float32), csum[..., sl]], axis=1)
        win = cp[:, w + L - T:w + L] - cp[:, L - T:L]
        cnt = jnp.minimum(pos + 1, w).astype(jnp.float32)
        groups.append(win / cnt[None, :, None] - uf[:, L - T:, sl])
    d = jnp.stack(groups, axis=2)
    mixed = jnp.einsum('btgc,gcd->btgd', d, w_pool.astype(jnp.float32)).reshape(B, T, D_MODEL)
    mixed = mixed * pool_scale.astype(jnp.float32)
    return mixed.astype(u.dtype), u_ext[:, L - POOL_BUF:]


def rglru_mixer(u, conv_buf, h_prev, q_start, w_gate, w_x, conv_w, conv_b, w_a, b_a, w_i, b_i, lam, w_out):
    B, T, _ = u.shape
    gate = jax.nn.gelu(u @ w_gate)
    xb = u @ w_x
    if conv_buf is None:
        conv_buf = jnp.zeros((B, CONV_WIDTH - 1, D_RNN), xb.dtype)
    xe = jnp.concatenate([conv_buf.astype(xb.dtype), xb], axis=1)
    xc = conv_b
    for tap in range(CONV_WIDTH):
        xc = xc + xe[:, tap:tap + T] * conv_w[tap]
    xf = xc.astype(jnp.float32)
    xh = xf.reshape(B, T, RG_HEADS, RG_BLOCK)
    r = jax.nn.sigmoid(jnp.einsum('bthi,hij->bthj', xh, w_a.astype(jnp.float32)) + b_a.astype(jnp.float32)).reshape(B, T, D_RNN)
    ig = jax.nn.sigmoid(jnp.einsum('bthi,hij->bthj', xh, w_i.astype(jnp.float32)) + b_i.astype(jnp.float32)).reshape(B, T, D_RNN)
    log_a = -RG_C * r * jax.nn.softplus(-lam.astype(jnp.float32))
    a = jnp.exp(log_a)
    mult = jnp.sqrt(-jnp.expm1(2.0 * log_a))
    pos = q_start + jnp.arange(T)
    mult = jnp.where((pos == 0)[None, :, None], 1.0, mult)
    b = mult * ig * xf
    h0 = jnp.zeros((B, D_RNN), jnp.float32) if h_prev is None else h_prev.astype(jnp.float32)

    def step(h, ab):
        h = ab[0] * h + ab[1]
        return h, h

    h_last, hs = lax.scan(step, h0, (jnp.swapaxes(a, 0, 1), jnp.swapaxes(b, 0, 1)))
    y = jnp.swapaxes(hs, 0, 1).astype(u.dtype) * gate
    return y @ w_out, xe[:, -(CONV_WIDTH - 1):], h_last


def gather_pages(pool, page_table):
    g = pool[page_table]
    return g.reshape(page_table.shape[0], page_table.shape[1] * PAGE_SIZE, N_HEADS, HEAD_DIM)


def _trunk(h, q_start, paged, pool_bufs, conv_bufs, h_inits, norm_g, attn_w_qkv, attn_w_o, attn_bias, pool_w,
           pool_scale, rg_w_gate, rg_w_x, rg_conv_w, rg_conv_b, rg_w_a, rg_b_a, rg_w_i, rg_b_i, rg_lambda,
           rg_w_out, mlp_w_up, mlp_w_down):
    new_k, new_v, new_pool, new_conv, new_h = [], [], [], [], []
    for layer in range(DEPTH):
        kind = layer % N_MIXERS
        j = layer // N_MIXERS
        g = norm_g[layer]
        u = rms_norm(h, g[0])
        if kind == 0:
            if paged is None:
                k_past = v_past = None
            else:
                ck, cv, pt = paged
                k_past = gather_pages(ck[j], pt)
                v_past = gather_pages(cv[j], pt)
            m, k_new, v_new = attn_mixer(u, attn_w_qkv[j], attn_w_o[j], attn_bias[j], k_past, v_past, q_start)
            new_k.append(k_new)
            new_v.append(v_new)
        elif kind == 1:
            buf = None if pool_bufs is None else pool_bufs[j]
            m, nb = pool_mixer(u, buf, q_start, pool_w[j], pool_scale[j])
            new_pool.append(nb)
        else:
            cb = None if conv_bufs is None else conv_bufs[j]
            hp = None if h_inits is None else h_inits[j]
            m, nc, nh = rglru_mixer(u, cb, hp, q_start, rg_w_gate[j], rg_w_x[j], rg_conv_w[j], rg_conv_b[j],
                                    rg_w_a[j], rg_b_a[j], rg_w_i[j], rg_b_i[j], rg_lambda[j], rg_w_out[j])
            new_conv.append(nc)
            new_h.append(nh)
        h = h + rms_norm(m, g[1])
        u = rms_norm(h, g[2])
        f = jnp.square(jax.nn.relu(u @ mlp_w_up[layer])) @ mlp_w_down[layer]
        h = h + rms_norm(f, g[3])
    return h, jnp.stack(new_k), jnp.stack(new_v), jnp.stack(new_pool), jnp.stack(new_conv), jnp.stack(new_h)


def setup_inputs(seed: int = 0) -> dict:
    key = jax.random.key(seed)
    ks = jax.random.split(key, 27)
    n_pages = PAST_LEN // PAGE_SIZE
    n_used = DEC_BATCH * n_pages
    n_phys = (n_used * 5) // 4

    def nrm(k, shape, s):
        return jax.random.normal(k, shape, jnp.float32) * s

    x_prompt = nrm(ks[0], (BATCH, SEQ, D_MODEL), 1.0)
    x_sample = nrm(ks[1], (DEC_BATCH, DEC_SEQ, D_MODEL), 1.0)
    cache_k = nrm(ks[2], (N_ATTN_LAYERS, n_phys, PAGE_SIZE, N_HEADS, HEAD_DIM), 1.0)
    cache_v = nrm(ks[3], (N_ATTN_LAYERS, n_phys, PAGE_SIZE, N_HEADS, HEAD_DIM), 1.0)
    page_table = jax.random.permutation(ks[4], n_phys)[:n_used].reshape(DEC_BATCH, n_pages).astype(jnp.int32)
    state_pool = nrm(ks[5], (N_POOLMIX_LAYERS, DEC_BATCH, POOL_BUF, D_MODEL), 1.0)
    state_conv = nrm(ks[6], (N_RGLRU_LAYERS, DEC_BATCH, CONV_WIDTH - 1, D_RNN), 1.0)
    state_h = nrm(ks[7], (N_RGLRU_LAYERS, DEC_BATCH, D_RNN), 0.5)
    meta_tokens = nrm(ks[8], (N_META, D_MODEL), 1.0)
    norm_g = 1.0 + nrm(ks[9], (DEPTH, 4, D_MODEL), 0.05)
    attn_w_qkv = nrm(ks[10], (N_ATTN_LAYERS, D_MODEL, 3 * D_MODEL), D_MODEL ** -0.5)
    attn_w_o = nrm(ks[11], (N_ATTN_LAYERS, D_MODEL, D_MODEL), D_MODEL ** -0.5)
    attn_bias = (jnp.linspace(SB_BIAS_NEAR, SB_BIAS_FAR, N_HEADS, dtype=jnp.float32)[None]
                 + nrm(ks[26], (N_ATTN_LAYERS, N_HEADS), 0.1))
    pool_w = nrm(ks[12], (N_POOLMIX_LAYERS, N_POOL_GROUPS, POOL_GROUP, POOL_GROUP), POOL_GROUP ** -0.5)
    pool_scale = 1.0 + nrm(ks[13], (N_POOLMIX_LAYERS, D_MODEL), 0.1)
    rg_w_gate = nrm(ks[14], (N_RGLRU_LAYERS, D_MODEL, D_RNN), D_MODEL ** -0.5)
    rg_w_x = nrm(ks[15], (N_RGLRU_LAYERS, D_MODEL, D_RNN), D_MODEL ** -0.5)
    rg_conv_w = nrm(ks[16], (N_RGLRU_LAYERS, CONV_WIDTH, D_RNN), CONV_WIDTH ** -0.5)
    rg_conv_b = nrm(ks[17], (N_RGLRU_LAYERS, D_RNN), 0.01)
    rg_w_a = nrm(ks[18], (N_RGLRU_LAYERS, RG_HEADS, RG_BLOCK, RG_BLOCK), RG_BLOCK ** -0.5)
    rg_b_a = nrm(ks[19], (N_RGLRU_LAYERS, RG_HEADS, RG_BLOCK), 0.01)
    rg_w_i = nrm(ks[20], (N_RGLRU_LAYERS, RG_HEADS, RG_BLOCK, RG_BLOCK), RG_BLOCK ** -0.5)
    rg_b_i = nrm(ks[21], (N_RGLRU_LAYERS, RG_HEADS, RG_BLOCK), 0.01)
    a_c = jax.random.uniform(ks[22], (N_RGLRU_LAYERS, D_RNN), jnp.float32, 0.9, 0.999)
    s0 = a_c ** (1.0 / RG_C)
    rg_lambda = jnp.log(s0) - jnp.log1p(-s0)
    rg_w_out = nrm(ks[23], (N_RGLRU_LAYERS, D_RNN, D_MODEL), D_RNN ** -0.5)
    mlp_w_up = nrm(ks[24], (DEPTH, D_MODEL, D_FF), D_MODEL ** -0.5)
    mlp_w_down = nrm(ks[25], (DEPTH, D_FF, D_MODEL), D_FF ** -0.5)
    return {'x_prompt': x_prompt, 'x_sample': x_sample, 'cache_k': cache_k, 'cache_v': cache_v,
            'page_table': page_table, 'state_pool': state_pool, 'state_conv': state_conv, 'state_h': state_h,
            'meta_tokens': meta_tokens, 'norm_g': norm_g, 'attn_w_qkv': attn_w_qkv, 'attn_w_o': attn_w_o,
            'attn_bias': attn_bias, 'pool_w': pool_w, 'pool_scale': pool_scale, 'rg_w_gate': rg_w_gate,
            'rg_w_x': rg_w_x, 'rg_conv_w': rg_conv_w, 'rg_conv_b': rg_conv_b, 'rg_w_a': rg_w_a,
            'rg_b_a': rg_b_a, 'rg_w_i': rg_w_i, 'rg_b_i': rg_b_i, 'rg_lambda': rg_lambda,
            'rg_w_out': rg_w_out, 'mlp_w_up': mlp_w_up, 'mlp_w_down': mlp_w_down}


def reference(x_prompt, x_sample, cache_k, cache_v, page_table, state_pool, state_conv, state_h,
              meta_tokens, norm_g, attn_w_qkv, attn_w_o, attn_bias, pool_w, pool_scale, rg_w_gate, rg_w_x,
              rg_conv_w, rg_conv_b, rg_w_a, rg_b_a, rg_w_i, rg_b_i, rg_lambda, rg_w_out,
              mlp_w_up, mlp_w_down):
    weights = (norm_g, attn_w_qkv, attn_w_o, attn_bias, pool_w, pool_scale, rg_w_gate, rg_w_x, rg_conv_w,
               rg_conv_b, rg_w_a, rg_b_a, rg_w_i, rg_b_i, rg_lambda, rg_w_out, mlp_w_up, mlp_w_down)
    meta = jnp.broadcast_to(meta_tokens.astype(x_prompt.dtype)[None], (x_prompt.shape[0], N_META, D_MODEL))
    h_p = jnp.concatenate([meta, x_prompt], axis=1)
    out_p, k_prompt, v_prompt, pool_prompt, conv_prompt, h_prompt = _trunk(
        h_p, 0, None, None, None, None, *weights)
    y_prompt = out_p[:, N_META:]
    past_len = page_table.shape[1] * PAGE_SIZE
    y_sample, k_sample, v_sample, pool_sample, conv_sample, h_sample = _trunk(
        x_sample, past_len, (cache_k, cache_v, page_table), state_pool, state_conv, state_h, *weights)
    return (y_prompt, y_sample, k_prompt, v_prompt, k_sample, v_sample, pool_prompt, pool_sample,
            conv_prompt, conv_sample, h_prompt, h_sample)
```

```python
import functools
import math

import jax
import jax.numpy as jnp
from jax import lax
from jax.experimental import pallas as pl
from jax.experimental.pallas import tpu as pltpu

F32 = jnp.float32
BF16 = jnp.bfloat16

D_MODEL = 1024
N_HEADS = 16
HEAD_DIM = D_MODEL // N_HEADS
N_PAIRS = N_HEADS // 2
D_FF = 4 * D_MODEL
N_META = 16
N_MIXERS = 3
DEPTH = 4
PAGE_SIZE = 128
POOL_WINDOWS = (2, 4, 8, 16)
POOL_GROUP = D_MODEL // len(POOL_WINDOWS)
POOL_BUF = max(POOL_WINDOWS) - 1
RG_HEADS = 4
RG_BLOCK = D_MODEL // RG_HEADS
CONV_WIDTH = 4
RG_C = 8.0
RMS_EPS = 1e-6

Q_BLOCK = 128
ROW_TILE = 512
TIME_TILE = 384
FF_CHUNK = 1024
VMEM_LIMIT = 56 * 1024 * 1024


def _params(n_axes, vmem=VMEM_LIMIT):
    return pltpu.CompilerParams(dimension_semantics=("arbitrary",) * n_axes, vmem_limit_bytes=vmem)


def _rms(x, g):
    ms = jnp.mean(x * x, axis=-1, keepdims=True)
    return x * lax.rsqrt(ms + RMS_EPS) * g


def _softplus(x):
    return jnp.maximum(x, 0.0) + jnp.log1p(jnp.exp(-jnp.abs(x)))


def _full(shape):
    nd = len(shape)
    return pl.BlockSpec(shape, lambda *_: (0,) * nd)


def _qkv_kernel(x_ref, g_ref, w_ref, kv_ref, qkvb_ref):
    u = _rms(x_ref[...], g_ref[...]).astype(BF16)
    y = jnp.dot(u, w_ref[...], preferred_element_type=F32)
    kv_ref[...] = y[:, D_MODEL:]
    qkvb_ref[:, :D_MODEL] = (y[:, :D_MODEL] * (HEAD_DIM ** -0.5)).astype(BF16)
    qkvb_ref[:, D_MODEL:] = y[:, D_MODEL:].astype(BF16)


def _qkv_call(h, g, w):
    n = h.shape[0]
    tm = min(ROW_TILE, n)
    return pl.pallas_call(
        _qkv_kernel,
        grid=(n // tm,),
        in_specs=[pl.BlockSpec((tm, D_MODEL), lambda i: (i, 0)), _full((1, D_MODEL)),
                  _full((D_MODEL, 3 * D_MODEL))],
        out_specs=[pl.BlockSpec((tm, 2 * D_MODEL), lambda i: (i, 0)),
                   pl.BlockSpec((tm, 3 * D_MODEL), lambda i: (i, 0))],
        out_shape=[jax.ShapeDtypeStruct((n, 2 * D_MODEL), F32),
                   jax.ShapeDtypeStruct((n, 3 * D_MODEL), BF16)],
        compiler_params=_params(1),
        name="qkv_proj",
    )(h, g, w)


def _suffix_matrix(n):
    j = jnp.arange(n)[:, None]
    s = jnp.arange(n)[None, :]
    tri = jnp.where(j > s, -1.0, 0.0)
    return jnp.concatenate([tri, jnp.full((n, n), -1.0)], axis=1).astype(BF16)


def _stick_weights(z, wneg, carry, mask):
    n = z.shape[1]
    sp = _softplus(z)
    logb = z - sp
    if mask is not None:
        sp = jnp.where(mask, sp, 0.0)
    hi = sp.astype(BF16)
    lo = (sp - hi.astype(F32)).astype(BF16)
    r = (jnp.dot(hi, wneg, preferred_element_type=F32)
         + jnp.dot(lo, wneg, preferred_element_type=F32))
    w = jnp.exp(logb + r[:, :n] + carry)
    if mask is not None:
        w = jnp.where(mask, w, 0.0)
    return w, carry + r[:, n:]


def _attn_prompt_kernel(bias_ref, q_ref, k_ref, v_ref, wneg_ref, o_ref):
    p = pl.program_id(1)
    qb = pl.program_id(2)
    qb_rows = 2 * Q_BLOCK
    lane = lax.broadcasted_iota(jnp.int32, (Q_BLOCK, 128), 1)
    q = q_ref[...].astype(F32)
    q2 = jnp.concatenate([jnp.where(lane < HEAD_DIM, q, 0.0), jnp.where(lane >= HEAD_DIM, q, 0.0)],
                         axis=0).astype(BF16)
    row = lax.broadcasted_iota(jnp.int32, (qb_rows, Q_BLOCK), 0)
    col = lax.broadcasted_iota(jnp.int32, (qb_rows, Q_BLOCK), 1)
    bias_t = jnp.where(row < Q_BLOCK, bias_ref[2 * p], bias_ref[2 * p + 1])
    wneg = wneg_ref[...]

    def block(kb, carry, acc, mask):
        start = pl.multiple_of(kb * Q_BLOCK, Q_BLOCK)
        k = k_ref[pl.ds(start, Q_BLOCK), :]
        v = v_ref[pl.ds(start, Q_BLOCK), :]
        z = lax.dot_general(q2, k, (((1,), (1,)), ((), ())), preferred_element_type=F32) + bias_t
        w, carry = _stick_weights(z, wneg, carry, mask)
        return carry, acc + jnp.dot(w.astype(BF16), v, preferred_element_type=F32)

    zeros = jnp.zeros((qb_rows, Q_BLOCK), F32)
    diag_mask = col < (row & (Q_BLOCK - 1))
    carry, acc = block(qb, zeros, zeros, diag_mask)
    carry, acc = lax.fori_loop(1, qb + 1, lambda i, c: block(qb - i, c[0], c[1], None), (carry, acc))
    o_ref[...] = jnp.where(lane < HEAD_DIM, acc[:Q_BLOCK], acc[Q_BLOCK:]).astype(BF16)


def _attn_prompt_call(qkvb, bias, wneg, batch, t_pad):
    nqb = t_pad // Q_BLOCK
    return pl.pallas_call(
        _attn_prompt_kernel,
        grid=(batch, N_PAIRS, nqb),
        in_specs=[pl.BlockSpec(memory_space=pltpu.SMEM),
                  pl.BlockSpec((Q_BLOCK, 128), lambda b, p, i: (b * nqb + i, p)),
                  pl.BlockSpec((t_pad, 128), lambda b, p, i: (b, N_PAIRS + p)),
                  pl.BlockSpec((t_pad, 128), lambda b, p, i: (b, 2 * N_PAIRS + p)),
                  _full((Q_BLOCK, 2 * Q_BLOCK))],
        out_specs=pl.BlockSpec((Q_BLOCK, 128), lambda b, p, i: (b * nqb + i, p)),
        out_shape=jax.ShapeDtypeStruct((batch * t_pad, D_MODEL), BF16),
        compiler_params=_params(3),
        name="attn_prompt",
    )(bias, qkvb, qkvb, qkvb, wneg)


def _attn_sample_kernel(pt_ref, q_ref, k_ref, v_ref, bias_ref, wneg_ref, o_ref, acc_ref, carry_ref):
    pg = pl.program_id(1)

    @pl.when(pg == 0)
    def _():
        acc_ref[...] = jnp.zeros_like(acc_ref)
        carry_ref[...] = jnp.zeros_like(carry_ref)

    head_of_lane = lax.broadcasted_iota(jnp.int32, (N_HEADS, D_MODEL), 1) // HEAD_DIM
    own = head_of_lane == lax.broadcasted_iota(jnp.int32, (N_HEADS, D_MODEL), 0)
    q = jnp.broadcast_to(q_ref[0].astype(F32), (N_HEADS, D_MODEL))
    qbd = jnp.where(own, q, 0.0).astype(BF16)
    k = k_ref[...].astype(BF16)
    v = v_ref[...].astype(BF16)
    z = lax.dot_general(qbd, k, (((1,), (1,)), ((), ())), preferred_element_type=F32) + bias_ref[...]
    w, carry = _stick_weights(z, wneg_ref[...], carry_ref[...], None)
    carry_ref[...] = carry
    acc_ref[...] += jnp.dot(w.astype(BF16), v, preferred_element_type=F32)

    @pl.when(pg == pl.num_programs(1) - 1)
    def _():
        o_ref[0] = jnp.sum(jnp.where(own, acc_ref[...], 0.0), axis=0, keepdims=True).astype(BF16)


def _attn_sample_call(page_table, q, cache_k, cache_v, page_base, bias_t, wneg):
    n_seq, n_pages = page_table.shape

    def page(b, pg, pt):
        return (page_base + pt[b, n_pages - 1 - pg], 0, 0)

    return pl.pallas_call(
        _attn_sample_kernel,
        grid_spec=pltpu.PrefetchScalarGridSpec(
            num_scalar_prefetch=1,
            grid=(n_seq, n_pages),
            in_specs=[pl.BlockSpec((1, 1, D_MODEL), lambda b, pg, pt: (b, 0, 0)),
                      pl.BlockSpec((None, PAGE_SIZE, D_MODEL), page),
                      pl.BlockSpec((None, PAGE_SIZE, D_MODEL), page),
                      pl.BlockSpec((N_HEADS, PAGE_SIZE), lambda b, pg, pt: (0, 0)),
                      pl.BlockSpec((PAGE_SIZE, 2 * PAGE_SIZE), lambda b, pg, pt: (0, 0))],
            out_specs=pl.BlockSpec((1, 1, D_MODEL), lambda b, pg, pt: (b, 0, 0)),
            scratch_shapes=[pltpu.VMEM((N_HEADS, D_MODEL), F32), pltpu.VMEM((N_HEADS, PAGE_SIZE), F32)]),
        out_shape=jax.ShapeDtypeStruct((n_seq, 1, D_MODEL), BF16),
        compiler_params=_params(2),
        name="attn_sample",
    )(page_table, q, cache_k, cache_v, bias_t, wneg)


def _proj_res_kernel(h_ref, o_ref, w_ref, g_ref, out_ref):
    m = jnp.dot(o_ref[...], w_ref[...], preferred_element_type=F32)
    out_ref[...] = h_ref[...] + _rms(m, g_ref[...])


def _proj_res_call(h, o, w, g):
    n = h.shape[0]
    tm = min(ROW_TILE, n)
    row = pl.BlockSpec((tm, D_MODEL), lambda i: (i, 0))
    return pl.pallas_call(
        _proj_res_kernel,
        grid=(n // tm,),
        in_specs=[row, row, _full((D_MODEL, D_MODEL)), _full((1, D_MODEL))],
        out_specs=row,
        out_shape=jax.ShapeDtypeStruct((n, D_MODEL), F32),
        compiler_params=_params(1),
        name="attn_out_proj",
    )(h, o, w, g)


def _mlp_kernel(h_ref, g_pre_ref, g_post_ref, wu_ref, wd_ref, out_ref):
    x = h_ref[...]
    u = _rms(x, g_pre_ref[...]).astype(BF16)
    f = jnp.zeros(x.shape, F32)
    for c in range(D_FF // FF_CHUNK):
        sl = slice(c * FF_CHUNK, (c + 1) * FF_CHUNK)
        a = jnp.maximum(jnp.dot(u, wu_ref[:, sl], preferred_element_type=F32), 0.0)
        f = f + jnp.dot((a * a).astype(BF16), wd_ref[sl, :], preferred_element_type=F32)
    out_ref[...] = x + _rms(f, g_post_ref[...])


def _mlp_call(h, g_pre, g_post, wu, wd):
    n = h.shape[0]
    tm = min(ROW_TILE, n)
    row = pl.BlockSpec((tm, D_MODEL), lambda i: (i, 0))
    return pl.pallas_call(
        _mlp_kernel,
        grid=(n // tm,),
        in_specs=[row, _full((1, D_MODEL)), _full((1, D_MODEL)),
                  pl.BlockSpec((D_MODEL, D_FF), lambda i: (0, 0), pipeline_mode=pl.Buffered(1)),
                  pl.BlockSpec((D_FF, D_MODEL), lambda i: (0, 0), pipeline_mode=pl.Buffered(1))],
        out_specs=row,
        out_shape=jax.ShapeDtypeStruct((n, D_MODEL), F32),
        compiler_params=_params(1),
        name="mlp",
    )(h, g_pre, g_post, wu, wd)


def _pool_mix(u, shifted, cnt_of, wp_ref, ps):
    groups = []
    for gi, w in enumerate(POOL_WINDOWS):
        sl = slice(gi * POOL_GROUP, (gi + 1) * POOL_GROUP)
        ug = u[:, sl]
        win = ug
        for j in range(1, w):
            win = win + shifted(j, sl)
        d = win / cnt_of(w) - ug
        groups.append(jnp.dot(d.astype(BF16), wp_ref[gi], preferred_element_type=F32))
    return jnp.concatenate(groups, axis=1) * ps


def _pool_prompt_kernel(tail_tile, tail_off, h_ref, g_pre_ref, g_post_ref, wp_ref, ps_ref,
                        out_ref, tail_ref, ext_ref):
    t = pl.program_id(1)
    tt = h_ref.shape[0]

    @pl.when(t == 0)
    def _():
        ext_ref[0:POOL_BUF + 1, :] = jnp.zeros((POOL_BUF + 1, D_MODEL), F32)

    x = h_ref[...]
    u = _rms(x, g_pre_ref[...])
    hist = POOL_BUF + 1
    ext_ref[hist:hist + tt, :] = u
    pos = t * tt + lax.broadcasted_iota(jnp.int32, (tt, 1), 0)
    m = _pool_mix(u, lambda j, sl: ext_ref[hist - j:hist - j + tt, sl],
                  lambda w: jnp.minimum(pos + 1, w).astype(F32), wp_ref, ps_ref[...])
    out_ref[...] = x + _rms(m, g_post_ref[...])
    ext_ref[0:hist, :] = u[tt - hist:, :]

    @pl.when(t == tail_tile)
    def _():
        tail_ref[...] = u[tail_off:tail_off + hist, :]


def _pool_prompt_call(h, g_pre, g_post, wp, ps, batch, t_pad, t_real):
    tt = TIME_TILE
    nt = t_pad // tt
    hist = POOL_BUF + 1
    tail_start = t_real - hist
    kern = functools.partial(_pool_prompt_kernel, tail_start // tt, tail_start % tt)
    return pl.pallas_call(
        kern,
        grid=(batch, nt),
        in_specs=[pl.BlockSpec((tt, D_MODEL), lambda b, t: (b * nt + t, 0)),
                  _full((1, D_MODEL)), _full((1, D_MODEL)),
                  _full((len(POOL_WINDOWS), POOL_GROUP, POOL_GROUP)), _full((1, D_MODEL))],
        out_specs=[pl.BlockSpec((tt, D_MODEL), lambda b, t: (b * nt + t, 0)),
                   pl.BlockSpec((None, hist, D_MODEL), lambda b, t: (b, 0, 0))],
        out_shape=[jax.ShapeDtypeStruct((batch * t_pad, D_MODEL), F32),
                   jax.ShapeDtypeStruct((batch, hist, D_MODEL), F32)],
        scratch_shapes=[pltpu.VMEM((hist + tt, D_MODEL), F32)],
        compiler_params=_params(2),
        name="pool_prompt",
    )(h, g_pre, g_post, wp, ps)


def _pool_sample_kernel(h_ref, g_pre_ref, g_post_ref, wp_ref, ps_ref, st_ref, out_ref, u_ref):
    x = h_ref[...]
    u = _rms(x, g_pre_ref[...])
    u_ref[...] = u
    m = _pool_mix(u, lambda j, sl: st_ref[POOL_BUF - j, :, sl], lambda w: float(w), wp_ref, ps_ref[...])
    out_ref[...] = x + _rms(m, g_post_ref[...])


def _pool_sample_call(h, g_pre, g_post, wp, ps, state_t):
    n = h.shape[0]
    return pl.pallas_call(
        _pool_sample_kernel,
        out_shape=[jax.ShapeDtypeStruct((n, D_MODEL), F32), jax.ShapeDtypeStruct((n, D_MODEL), F32)],
        compiler_params=pltpu.CompilerParams(vmem_limit_bytes=VMEM_LIMIT),
        name="pool_sample",
    )(h, g_pre, g_post, wp, ps, state_t)


def _gelu_tanh(x):
    return 0.5 * x * (1.0 + jnp.tanh(math.sqrt(2.0 / math.pi) * (x + 0.044715 * (x * x * x))))


def _rg_branches(x, g_pre, wgx_ref):
    u = _rms(x, g_pre).astype(BF16)
    gx = jnp.dot(u, wgx_ref[...], preferred_element_type=F32)
    return _gelu_tanh(gx[:, :D_MODEL]), gx[:, D_MODEL:]


def _rg_gates(xc, wa_ref, ba, wi_ref, bi, lam):
    xcb = xc.astype(BF16)
    ra, ri = [], []
    for hd in range(RG_HEADS):
        sl = slice(hd * RG_BLOCK, (hd + 1) * RG_BLOCK)
        ra.append(jnp.dot(xcb[:, sl], wa_ref[hd], preferred_element_type=F32))
        ri.append(jnp.dot(xcb[:, sl], wi_ref[hd], preferred_element_type=F32))
    r = jax.nn.sigmoid(jnp.concatenate(ra, axis=1) + ba)
    ig = jax.nn.sigmoid(jnp.concatenate(ri, axis=1) + bi)
    log_a = -RG_C * r * _softplus(-lam)
    a = jnp.exp(log_a)
    mult = jnp.sqrt(-jnp.tanh(log_a) * (a * a + 1.0))
    return a, mult, ig * xc


def _rg_prompt_kernel(tail_tile, tail_off, h_ref, g_pre_ref, g_post_ref, wgx_ref, cw_ref, cb_ref, wa_ref,
                      ba_ref, wi_ref, bi_ref, lam_ref, wo_ref, out_ref, ctail_ref, htail_ref,
                      xext_ref, hc_ref):
    t = pl.program_id(1)
    tt = h_ref.shape[0]
    hist = 8

    @pl.when(t == 0)
    def _():
        xext_ref[0:hist, :] = jnp.zeros((hist, D_MODEL), F32)
        hc_ref[...] = jnp.zeros_like(hc_ref)

    x = h_ref[...]
    gate, xb = _rg_branches(x, g_pre_ref[...], wgx_ref)
    xext_ref[hist:hist + tt, :] = xb
    xc = cb_ref[...]
    for tap in range(CONV_WIDTH):
        off = hist - (CONV_WIDTH - 1) + tap
        xc = xc + xext_ref[off:off + tt, :] * cw_ref[tap:tap + 1, :]
    a, mult, gx = _rg_gates(xc, wa_ref, ba_ref[...], wi_ref, bi_ref[...], lam_ref[...])
    row = lax.broadcasted_iota(jnp.int32, (tt, 1), 0)
    mult = jnp.where((row == 0) & (t == 0), 1.0, mult)
    b = mult * gx
    k = 1
    while k < tt:
        keep = row >= k
        a_prev = jnp.where(keep, pltpu.roll(a, k, 0), 1.0)
        b_prev = jnp.where(keep, pltpu.roll(b, k, 0), 0.0)
        b = b + a * b_prev
        a = a * a_prev
        k *= 2
    hs = b + a * hc_ref[...]
    hc_ref[...] = hs[tt - 1:tt, :]
    y = (hs * gate).astype(BF16)
    out_ref[...] = x + _rms(jnp.dot(y, wo_ref[...], preferred_element_type=F32), g_post_ref[...])
    xext_ref[0:hist, :] = xb[tt - hist:, :]

    @pl.when(t == tail_tile)
    def _():
        ctail_ref[...] = xb[tail_off:tail_off + hist, :]
        htail_ref[...] = hs[tail_off:tail_off + hist, :]


def _rg_weight_specs():
    return [_full((1, D_MODEL)), _full((1, D_MODEL)), _full((D_MODEL, 2 * D_MODEL)),
            _full((CONV_WIDTH, D_MODEL)), _full((1, D_MODEL)),
            _full((RG_HEADS, RG_BLOCK, RG_BLOCK)), _full((1, D_MODEL)),
            _full((RG_HEADS, RG_BLOCK, RG_BLOCK)), _full((1, D_MODEL)),
            _full((1, D_MODEL)), _full((D_MODEL, D_MODEL))]


def _rg_prompt_call(h, weights, batch, t_pad, t_real):
    tt = TIME_TILE
    nt = t_pad // tt
    hist = 8
    tail_start = t_real - hist
    kern = functools.partial(_rg_prompt_kernel, tail_start // tt, tail_start % tt)
    tail = pl.BlockSpec((None, hist, D_MODEL), lambda b, t: (b, 0, 0))
    return pl.pallas_call(
        kern,
        grid=(batch, nt),
        in_specs=[pl.BlockSpec((tt, D_MODEL), lambda b, t: (b * nt + t, 0))] + _rg_weight_specs(),
        out_specs=[pl.BlockSpec((tt, D_MODEL), lambda b, t: (b * nt + t, 0)), tail, tail],
        out_shape=[jax.ShapeDtypeStruct((batch * t_pad, D_MODEL), F32),
                   jax.ShapeDtypeStruct((batch, hist, D_MODEL), F32),
                   jax.ShapeDtypeStruct((batch, hist, D_MODEL), F32)],
        scratch_shapes=[pltpu.VMEM((hist + tt, D_MODEL), F32), pltpu.VMEM((1, D_MODEL), F32)],
        compiler_params=_params(2),
        name="rglru_prompt",
    )(h, *weights)


def _rg_sample_kernel(h_ref, g_pre_ref, g_post_ref, wgx_ref, cw_ref, cb_ref, wa_ref, ba_ref, wi_ref, bi_ref,
                      lam_ref, wo_ref, cs_ref, hp_ref, out_ref, xb_ref, hn_ref):
    x = h_ref[...]
    gate, xb = _rg_branches(x, g_pre_ref[...], wgx_ref)
    xb_ref[...] = xb
    xc = cb_ref[...]
    for tap in range(CONV_WIDTH - 1):
        xc = xc + cs_ref[tap] * cw_ref[tap:tap + 1, :]
    xc = xc + xb * cw_ref[CONV_WIDTH - 1:CONV_WIDTH, :]
    a, mult, gx = _rg_gates(xc, wa_ref, ba_ref[...], wi_ref, bi_ref[...], lam_ref[...])
    hn = a * hp_ref[...] + mult * gx
    hn_ref[...] = hn
    y = (hn * gate).astype(BF16)
    out_ref[...] = x + _rms(jnp.dot(y, wo_ref[...], preferred_element_type=F32), g_post_ref[...])


def _rg_sample_call(h, weights, conv_state_t, h_prev):
    n = h.shape[0]
    out = jax.ShapeDtypeStruct((n, D_MODEL), F32)
    return pl.pallas_call(
        _rg_sample_kernel,
        out_shape=[out, out, out],
        compiler_params=pltpu.CompilerParams(vmem_limit_bytes=VMEM_LIMIT),
        name="rglru_sample",
    )(h, *weights, conv_state_t, h_prev)


def kernel(x_prompt, x_sample, cache_k, cache_v, page_table, state_pool, state_conv, state_h, meta_tokens, norm_g, attn_w_qkv, attn_w_o, attn_bias, pool_w, pool_scale, rg_w_gate, rg_w_x, rg_conv_w, rg_conv_b, rg_w_a, rg_b_a, rg_w_i, rg_b_i, rg_lambda, rg_w_out, mlp_w_up, mlp_w_down):
    batch, seq, _ = x_prompt.shape
    n_seq = x_sample.shape[0]
    t_real = seq + N_META
    t_unit = math.lcm(Q_BLOCK, TIME_TILE)
    t_pad = -(-t_real // t_unit) * t_unit
    assert (batch * t_pad) % ROW_TILE == 0 and x_sample.shape[1] == 1
    n_phys = cache_k.shape[1]

    meta = jnp.broadcast_to(meta_tokens[None], (batch, N_META, D_MODEL))
    pad = jnp.zeros((batch, t_pad - t_real, D_MODEL), F32)
    h_p = jnp.concatenate([meta, x_prompt, pad], axis=1).reshape(batch * t_pad, D_MODEL)
    h_s = x_sample.reshape(n_seq, D_MODEL)
    cache_k2 = cache_k.reshape(-1, PAGE_SIZE, D_MODEL)
    cache_v2 = cache_v.reshape(-1, PAGE_SIZE, D_MODEL)
    wneg = _suffix_matrix(Q_BLOCK)
    row = lambda v: v.reshape(1, D_MODEL)

    k_p, v_p, k_s, v_s = [], [], [], []
    pool_p, pool_s, conv_p, conv_s, hl_p, hl_s = [], [], [], [], [], []
    for layer in range(DEPTH):
        kind, j = layer % N_MIXERS, layer // N_MIXERS
        g = norm_g[layer]
        if kind == 0:
            w_qkv = attn_w_qkv[j].astype(BF16)
            w_o = attn_w_o[j].astype(BF16)
            kv, qkvb = _qkv_call(h_p, row(g[0]), w_qkv)
            o = _attn_prompt_call(qkvb, attn_bias[j], wneg, batch, t_pad)
            h_p = _proj_res_call(h_p, o, w_o, row(g[1]))
            kv3 = kv.reshape(batch, t_pad, 2, N_HEADS, HEAD_DIM)[:, :t_real]
            k_p.append(kv3[:, :, 0])
            v_p.append(kv3[:, :, 1])

            kv, qkvb = _qkv_call(h_s, row(g[0]), w_qkv)
            bias_t = jnp.broadcast_to(attn_bias[j][:, None], (N_HEADS, PAGE_SIZE))
            o = _attn_sample_call(page_table, qkvb[:, :D_MODEL].reshape(n_seq, 1, D_MODEL), cache_k2, cache_v2,
                                  j * n_phys, bias_t, wneg)
            h_s = _proj_res_call(h_s, o.reshape(n_seq, D_MODEL), w_o, row(g[1]))
            kv3 = kv.reshape(n_seq, 1, 2, N_HEADS, HEAD_DIM)
            k_s.append(kv3[:, :, 0])
            v_s.append(kv3[:, :, 1])
        elif kind == 1:
            wp = pool_w[j].astype(BF16)
            ps = row(pool_scale[j])
            h_p, tail = _pool_prompt_call(h_p, row(g[0]), row(g[1]), wp, ps, batch, t_pad, t_real)
            pool_p.append(tail[:, 1:])
            st = state_pool[j]
            h_s, u_s = _pool_sample_call(h_s, row(g[0]), row(g[1]), wp, ps, jnp.swapaxes(st, 0, 1))
            pool_s.append(jnp.concatenate([st[:, 1:], u_s[:, None]], axis=1))
        else:
            weights = (row(g[0]), row(g[1]),
                       jnp.concatenate([rg_w_gate[j], rg_w_x[j]], axis=1).astype(BF16),
                       rg_conv_w[j], row(rg_conv_b[j]),
                       rg_w_a[j].astype(BF16), row(rg_b_a[j]), rg_w_i[j].astype(BF16), row(rg_b_i[j]),
                       row(rg_lambda[j]), rg_w_out[j].astype(BF16))
            h_p, ctail, htail = _rg_prompt_call(h_p, weights, batch, t_pad, t_real)
            conv_p.append(ctail[:, 8 - (CONV_WIDTH - 1):])
            hl_p.append(htail[:, 7])
            cs = state_conv[j]
            h_s, xb_s, hn_s = _rg_sample_call(h_s, weights, jnp.swapaxes(cs, 0, 1), state_h[j])
            conv_s.append(jnp.concatenate([cs[:, 1:], xb_s[:, None]], axis=1))
            hl_s.append(hn_s)
        w_up = mlp_w_up[layer].astype(BF16)
        w_down = mlp_w_down[layer].astype(BF16)
        h_p = _mlp_call(h_p, row(g[2]), row(g[3]), w_up, w_down)
        h_s = _mlp_call(h_s, row(g[2]), row(g[3]), w_up, w_down)

    y_prompt = h_p.reshape(batch, t_pad, D_MODEL)[:, N_META:t_real]
    y_sample = h_s.reshape(n_seq, 1, D_MODEL)
    return (y_prompt, y_sample, jnp.stack(k_p), jnp.stack(v_p), jnp.stack(k_s), jnp.stack(v_s),
            jnp.stack(pool_p), jnp.stack(pool_s), jnp.stack(conv_p), jnp.stack(conv_s),
            jnp.stack(hl_p), jnp.stack(hl_s))
```

```python
import functools
import math

import jax
import jax.numpy as jnp
import numpy as np
from jax import lax
from jax.experimental import pallas as pl
from jax.experimental.pallas import tpu as pltpu

F32 = jnp.float32
BF16 = jnp.bfloat16

D_MODEL = 1024
N_HEADS = 16
HEAD_DIM = D_MODEL // N_HEADS
N_PAIRS = N_HEADS // 2
D_FF = 4 * D_MODEL
N_META = 16
N_MIXERS = 3
DEPTH = 4
PAGE_SIZE = 128
POOL_WINDOWS = (2, 4, 8, 16)
POOL_GROUP = D_MODEL // len(POOL_WINDOWS)
POOL_BUF = max(POOL_WINDOWS) - 1
RG_HEADS = 4
RG_BLOCK = D_MODEL // RG_HEADS
CONV_WIDTH = 4
RG_C = 8.0
RMS_EPS = 1e-6
LOG2E = math.log2(math.e)
SIGN_BIT = np.int32(-2 ** 31)

LANES = 128
SUBLANES = 8
Q_BLOCK = 256
K_CHUNK = 256
PAGES_PER_STEP = 4
ROW_TILE = 512
TIME_TILE = 256
FF_CHUNK = 1024
VMEM_LIMIT = 56 * 1024 * 1024


def _params(n_axes, vmem=VMEM_LIMIT):
    return pltpu.CompilerParams(dimension_semantics=("arbitrary",) * n_axes, vmem_limit_bytes=vmem)


def _rms(x, g):
    ms = jnp.mean(x * x, axis=-1, keepdims=True)
    return x * lax.rsqrt(ms + RMS_EPS) * g


def _softplus(x):
    return jnp.maximum(x, 0.0) + jnp.log1p(jnp.exp(-jnp.abs(x)))


def _full(shape):
    nd = len(shape)
    return pl.BlockSpec(shape, lambda *_: (0,) * nd)


def _qkv_kernel(x_ref, g_ref, w_ref, kv_ref, qkvb_ref):
    u = _rms(x_ref[...], g_ref[...]).astype(BF16)
    y = jnp.dot(u, w_ref[...], preferred_element_type=F32)
    kv_ref[...] = y[:, D_MODEL:]
    qkvb_ref[:, :D_MODEL] = (y[:, :D_MODEL] * (LOG2E * HEAD_DIM ** -0.5)).astype(BF16)
    qkvb_ref[:, D_MODEL:2 * D_MODEL] = y[:, D_MODEL:2 * D_MODEL].astype(BF16)
    v = y[:, 2 * D_MODEL:]
    even = (lax.broadcasted_iota(jnp.int32, v.shape, 1) & HEAD_DIM) == 0
    qkvb_ref[:, 2 * D_MODEL:3 * D_MODEL] = jnp.where(even, v, 0.0).astype(BF16)
    qkvb_ref[:, 3 * D_MODEL:] = jnp.where(even, 0.0, v).astype(BF16)


def _qkv_call(h, g, w):
    n = h.shape[0]
    tm = min(ROW_TILE, n)
    return pl.pallas_call(
        _qkv_kernel,
        grid=(n // tm,),
        in_specs=[pl.BlockSpec((tm, D_MODEL), lambda i: (i, 0)), _full((1, D_MODEL)),
                  _full((D_MODEL, 3 * D_MODEL))],
        out_specs=[pl.BlockSpec((tm, 2 * D_MODEL), lambda i: (i, 0)),
                   pl.BlockSpec((tm, 4 * D_MODEL), lambda i: (i, 0))],
        out_shape=[jax.ShapeDtypeStruct((n, 2 * D_MODEL), F32),
                   jax.ShapeDtypeStruct((n, 4 * D_MODEL), BF16)],
        compiler_params=_params(1),
        name="qkv_proj",
    )(h, g, w)


def _suffix_matrix(n):
    j = jnp.arange(n)[:, None]
    s = jnp.arange(n)[None, :]
    return jnp.where(j > s, -1.0, 0.0).astype(BF16)


def _stick_prep(z2, mask):
    neg_abs = lax.bitcast_convert_type(lax.bitcast_convert_type(z2, jnp.int32) | SIGN_BIT, F32)
    sp = jnp.maximum(z2, 0.0) + jnp.log(1.0 + jnp.exp2(neg_abs)) * LOG2E
    logb = z2 - sp
    return logb, sp if mask is None else jnp.where(mask, sp, 0.0)


def _stick_weights(logb, sp, wneg, carry, mask):
    w = jnp.exp2(logb + jnp.dot(sp, wneg, preferred_element_type=F32) + carry)
    return w if mask is None else jnp.where(mask, w, 0.0)


def _attn_prompt_kernel(bias_ref, q_ref, k_ref, v0_ref, v1_ref, wneg_ref, o_ref,
                        logb_ref, sp_ref, tot_ref, w_ref, carry_ref, acc_ref):
    p = pl.program_id(1)
    qb = pl.program_id(2)
    m = 2 * Q_BLOCK
    lane = lax.broadcasted_iota(jnp.int32, (Q_BLOCK, LANES), 1)
    q = q_ref[...].astype(F32)
    q2 = jnp.concatenate([jnp.where(lane < HEAD_DIM, q, 0.0), jnp.where(lane >= HEAD_DIM, q, 0.0)],
                         axis=0).astype(BF16)
    row = lax.broadcasted_iota(jnp.int32, (m, K_CHUNK), 0)
    col = lax.broadcasted_iota(jnp.int32, (m, K_CHUNK), 1)
    bias_t = jnp.where(row < Q_BLOCK, bias_ref[2 * p], bias_ref[2 * p + 1])
    wneg = wneg_ref[...]

    def chunk(ref, kb):
        return ref[pl.ds(pl.multiple_of(jnp.maximum(kb, 0) * K_CHUNK, K_CHUNK), K_CHUNK), :]

    def logits(kb, mask):
        z = lax.dot_general(q2, chunk(k_ref, kb), (((1,), (1,)), ((), ())), preferred_element_type=F32)
        logb_ref[...], sp = _stick_prep(z + bias_t, mask)
        sp_ref[...] = sp.astype(BF16)
        tot_ref[...] = sp[:, :LANES] + sp[:, LANES:]

    def weights(mask):
        carry = carry_ref[...]
        w_ref[...] = _stick_weights(logb_ref[...], sp_ref[...], wneg, carry, mask).astype(BF16)
        carry_ref[...] = carry - jnp.sum(tot_ref[...], axis=1, keepdims=True)

    def weighted_values(kb):
        w2 = jnp.concatenate([w_ref[:Q_BLOCK, :], w_ref[Q_BLOCK:, :]], axis=1)
        v2 = jnp.concatenate([chunk(v0_ref, kb), chunk(v1_ref, kb)], axis=0)
        acc_ref[...] += jnp.dot(w2, v2, preferred_element_type=F32)

    diag_mask = col < (row & (Q_BLOCK - 1))
    carry_ref[...] = jnp.zeros_like(carry_ref)
    acc_ref[...] = jnp.zeros_like(acc_ref)
    logits(qb, diag_mask)
    weights(diag_mask)
    logits(qb - 1, None)

    def trip(kb):
        weighted_values(kb)
        weights(None)
        logits(kb - 2, None)

    def two_trips(i, _):
        trip(qb - 2 * i)
        trip(qb - 2 * i - 1)
        return 0

    n_trips = qb + 1
    lax.fori_loop(0, n_trips // 2, two_trips, 0)

    @pl.when(n_trips % 2 == 1)
    def _():
        trip(0)

    o_ref[...] = acc_ref[...].astype(BF16)


def _attn_prompt_call(qkvb, bias, wneg, batch, t_pad):
    assert Q_BLOCK == K_CHUNK == 2 * LANES and t_pad % Q_BLOCK == 0
    nqb = t_pad // Q_BLOCK
    m = 2 * Q_BLOCK

    def cols(section):
        return pl.BlockSpec((t_pad, LANES), lambda b, p, i: (b, section * N_PAIRS + p))

    return pl.pallas_call(
        _attn_prompt_kernel,
        grid=(batch, N_PAIRS, nqb),
        in_specs=[pl.BlockSpec(memory_space=pltpu.SMEM),
                  pl.BlockSpec((Q_BLOCK, LANES), lambda b, p, i: (b * nqb + i, p)),
                  cols(1), cols(2), cols(3), _full((K_CHUNK, K_CHUNK))],
        out_specs=pl.BlockSpec((Q_BLOCK, LANES), lambda b, p, i: (b * nqb + i, p)),
        out_shape=jax.ShapeDtypeStruct((batch * t_pad, D_MODEL), BF16),
        scratch_shapes=[pltpu.VMEM((m, K_CHUNK), F32), pltpu.VMEM((m, K_CHUNK), BF16), pltpu.VMEM((m, LANES), F32),
                        pltpu.VMEM((m, K_CHUNK), BF16), pltpu.VMEM((m, 1), F32),
                        pltpu.VMEM((Q_BLOCK, LANES), F32)],
        compiler_params=_params(3),
        name="attn_prompt",
    )(bias, qkvb, qkvb, qkvb, qkvb, wneg)


def _attn_sample_kernel(pt_ref, q_ref, bias_ref, wneg_ref, *refs):
    k_refs = refs[:PAGES_PER_STEP]
    v_refs = refs[PAGES_PER_STEP:2 * PAGES_PER_STEP]
    o_ref, acc_ref, carry_ref = refs[2 * PAGES_PER_STEP:]
    step = pl.program_id(1)

    @pl.when(step == 0)
    def _():
        acc_ref[...] = jnp.zeros_like(acc_ref)
        carry_ref[...] = jnp.zeros_like(carry_ref)

    def heads(ref):
        return jnp.stack([ref[pl.ds(h, PAGE_SIZE, stride=N_HEADS), :] for h in range(N_HEADS)]).astype(BF16)

    q = q_ref[...]
    wneg = wneg_ref[...]
    carry = carry_ref[...]
    acc = acc_ref[...]
    for k_ref, v_ref in zip(k_refs, v_refs):
        z = jnp.einsum("hqd,hkd->hqk", q, heads(k_ref), preferred_element_type=F32) + bias_ref[...]
        logb, sp = _stick_prep(z.reshape(N_HEADS * SUBLANES, PAGE_SIZE), None)
        w = _stick_weights(logb, sp.astype(BF16), wneg, carry, None)
        carry = carry - jnp.sum(sp, axis=1, keepdims=True)
        w = w.astype(BF16).reshape(N_HEADS, SUBLANES, PAGE_SIZE)
        acc = acc + jnp.einsum("hqk,hkd->hqd", w, heads(v_ref), preferred_element_type=F32)
    carry_ref[...] = carry
    acc_ref[...] = acc

    @pl.when(step == pl.num_programs(1) - 1)
    def _():
        o_ref[...] = acc.astype(BF16)


def _attn_sample_call(page_table, q, cache_k, cache_v, page_base, bias_t, wneg):
    n_seq, n_pages = page_table.shape
    assert n_pages % PAGES_PER_STEP == 0

    def page(j):
        return lambda b, s, pt: (page_base + pt[b, n_pages - 1 - (s * PAGES_PER_STEP + j)], 0, 0)

    page_specs = [pl.BlockSpec((None, PAGE_SIZE * N_HEADS, HEAD_DIM), page(j)) for j in range(PAGES_PER_STEP)]
    q_spec = pl.BlockSpec((None, N_HEADS, SUBLANES, HEAD_DIM), lambda b, s, pt: (b, 0, 0, 0))
    return pl.pallas_call(
        _attn_sample_kernel,
        grid_spec=pltpu.PrefetchScalarGridSpec(
            num_scalar_prefetch=1,
            grid=(n_seq, n_pages // PAGES_PER_STEP),
            in_specs=[q_spec,
                      pl.BlockSpec((N_HEADS, SUBLANES, PAGE_SIZE), lambda b, s, pt: (0, 0, 0)),
                      pl.BlockSpec((PAGE_SIZE, PAGE_SIZE), lambda b, s, pt: (0, 0))]
                     + page_specs + page_specs,
            out_specs=q_spec,
            scratch_shapes=[pltpu.VMEM((N_HEADS, SUBLANES, HEAD_DIM), F32),
                            pltpu.VMEM((N_HEADS * SUBLANES, 1), F32)]),
        out_shape=jax.ShapeDtypeStruct((n_seq, N_HEADS, SUBLANES, HEAD_DIM), BF16),
        compiler_params=_params(2),
        name="attn_sample",
    )(page_table, q, bias_t, wneg, *([cache_k] * PAGES_PER_STEP), *([cache_v] * PAGES_PER_STEP))


def _proj_res_kernel(h_ref, o_ref, w_ref, g_ref, out_ref):
    m = jnp.dot(o_ref[...], w_ref[...], preferred_element_type=F32)
    out_ref[...] = h_ref[...] + _rms(m, g_ref[...])


def _proj_res_call(h, o, w, g):
    n = h.shape[0]
    tm = min(ROW_TILE, n)
    row = pl.BlockSpec((tm, D_MODEL), lambda i: (i, 0))
    return pl.pallas_call(
        _proj_res_kernel,
        grid=(n // tm,),
        in_specs=[row, row, _full((D_MODEL, D_MODEL)), _full((1, D_MODEL))],
        out_specs=row,
        out_shape=jax.ShapeDtypeStruct((n, D_MODEL), F32),
        compiler_params=_params(1),
        name="attn_out_proj",
    )(h, o, w, g)


def _mlp_kernel(h_ref, g_pre_ref, g_post_ref, wu_ref, wd_ref, out_ref):
    x = h_ref[...]
    u = _rms(x, g_pre_ref[...]).astype(BF16)
    f = jnp.zeros(x.shape, F32)
    for c in range(D_FF // FF_CHUNK):
        sl = slice(c * FF_CHUNK, (c + 1) * FF_CHUNK)
        a = jnp.maximum(jnp.dot(u, wu_ref[:, sl], preferred_element_type=F32), 0.0)
        f = f + jnp.dot((a * a).astype(BF16), wd_ref[sl, :], preferred_element_type=F32)
    out_ref[...] = x + _rms(f, g_post_ref[...])


def _mlp_call(h, g_pre, g_post, wu, wd):
    n = h.shape[0]
    tm = min(ROW_TILE, n)
    row = pl.BlockSpec((tm, D_MODEL), lambda i: (i, 0))
    return pl.pallas_call(
        _mlp_kernel,
        grid=(n // tm,),
        in_specs=[row, _full((1, D_MODEL)), _full((1, D_MODEL)),
                  pl.BlockSpec((D_MODEL, D_FF), lambda i: (0, 0), pipeline_mode=pl.Buffered(1)),
                  pl.BlockSpec((D_FF, D_MODEL), lambda i: (0, 0), pipeline_mode=pl.Buffered(1))],
        out_specs=row,
        out_shape=jax.ShapeDtypeStruct((n, D_MODEL), F32),
        compiler_params=_params(1),
        name="mlp",
    )(h, g_pre, g_post, wu, wd)


def _pool_mix(u, shifted, cnt_of, wp_ref, ps):
    groups = []
    for gi, w in enumerate(POOL_WINDOWS):
        sl = slice(gi * POOL_GROUP, (gi + 1) * POOL_GROUP)
        ug = u[:, sl]
        win = ug
        for j in range(1, w):
            win = win + shifted(j, sl)
        d = win / cnt_of(w) - ug
        groups.append(jnp.dot(d.astype(BF16), wp_ref[gi], preferred_element_type=F32))
    return jnp.concatenate(groups, axis=1) * ps


def _pool_prompt_kernel(tail_tile, tail_off, h_ref, g_pre_ref, g_post_ref, wp_ref, ps_ref,
                        out_ref, tail_ref, ext_ref):
    t = pl.program_id(1)
    tt = h_ref.shape[0]

    @pl.when(t == 0)
    def _():
        ext_ref[0:POOL_BUF + 1, :] = jnp.zeros((POOL_BUF + 1, D_MODEL), F32)

    x = h_ref[...]
    u = _rms(x, g_pre_ref[...])
    hist = POOL_BUF + 1
    ext_ref[hist:hist + tt, :] = u
    pos = t * tt + lax.broadcasted_iota(jnp.int32, (tt, 1), 0)
    m = _pool_mix(u, lambda j, sl: ext_ref[hist - j:hist - j + tt, sl],
                  lambda w: jnp.minimum(pos + 1, w).astype(F32), wp_ref, ps_ref[...])
    out_ref[...] = x + _rms(m, g_post_ref[...])
    ext_ref[0:hist, :] = u[tt - hist:, :]

    @pl.when(t == tail_tile)
    def _():
        tail_ref[...] = u[tail_off:tail_off + hist, :]


def _pool_prompt_call(h, g_pre, g_post, wp, ps, batch, t_pad, t_real):
    tt = TIME_TILE
    nt = t_pad // tt
    hist = POOL_BUF + 1
    tail_start = t_real - hist
    kern = functools.partial(_pool_prompt_kernel, tail_start // tt, tail_start % tt)
    return pl.pallas_call(
        kern,
        grid=(batch, nt),
        in_specs=[pl.BlockSpec((tt, D_MODEL), lambda b, t: (b * nt + t, 0)),
                  _full((1, D_MODEL)), _full((1, D_MODEL)),
                  _full((len(POOL_WINDOWS), POOL_GROUP, POOL_GROUP)), _full((1, D_MODEL))],
        out_specs=[pl.BlockSpec((tt, D_MODEL), lambda b, t: (b * nt + t, 0)),
                   pl.BlockSpec((None, hist, D_MODEL), lambda b, t: (b, 0, 0))],
        out_shape=[jax.ShapeDtypeStruct((batch * t_pad, D_MODEL), F32),
                   jax.ShapeDtypeStruct((batch, hist, D_MODEL), F32)],
        scratch_shapes=[pltpu.VMEM((hist + tt, D_MODEL), F32)],
        compiler_params=_params(2),
        name="pool_prompt",
    )(h, g_pre, g_post, wp, ps)


def _pool_sample_kernel(h_ref, g_pre_ref, g_post_ref, wp_ref, ps_ref, st_ref, out_ref, u_ref):
    x = h_ref[...]
    u = _rms(x, g_pre_ref[...])
    u_ref[...] = u
    m = _pool_mix(u, lambda j, sl: st_ref[POOL_BUF - j, :, sl], lambda w: float(w), wp_ref, ps_ref[...])
    out_ref[...] = x + _rms(m, g_post_ref[...])


def _pool_sample_call(h, g_pre, g_post, wp, ps, state_t):
    n = h.shape[0]
    return pl.pallas_call(
        _pool_sample_kernel,
        out_shape=[jax.ShapeDtypeStruct((n, D_MODEL), F32), jax.ShapeDtypeStruct((n, D_MODEL), F32)],
        compiler_params=pltpu.CompilerParams(vmem_limit_bytes=VMEM_LIMIT),
        name="pool_sample",
    )(h, g_pre, g_post, wp, ps, state_t)


def _gelu_tanh(x):
    return 0.5 * x * (1.0 + jnp.tanh(math.sqrt(2.0 / math.pi) * (x + 0.044715 * (x * x * x))))


def _rg_branches(x, g_pre, wgx_ref):
    u = _rms(x, g_pre).astype(BF16)
    gx = jnp.dot(u, wgx_ref[...], preferred_element_type=F32)
    return _gelu_tanh(gx[:, :D_MODEL]), gx[:, D_MODEL:]


def _rg_gates(xc, wa_ref, ba, wi_ref, bi, lam):
    xcb = xc.astype(BF16)
    ra, ri = [], []
    for hd in range(RG_HEADS):
        sl = slice(hd * RG_BLOCK, (hd + 1) * RG_BLOCK)
        ra.append(jnp.dot(xcb[:, sl], wa_ref[hd], preferred_element_type=F32))
        ri.append(jnp.dot(xcb[:, sl], wi_ref[hd], preferred_element_type=F32))
    r = jax.nn.sigmoid(jnp.concatenate(ra, axis=1) + ba)
    ig = jax.nn.sigmoid(jnp.concatenate(ri, axis=1) + bi)
    log_a = -RG_C * r * _softplus(-lam)
    a = jnp.exp(log_a)
    mult = jnp.sqrt(-jnp.tanh(log_a) * (a * a + 1.0))
    return a, mult, ig * xc


def _rg_prompt_kernel(tail_tile, tail_off, h_ref, g_pre_ref, g_post_ref, wgx_ref, cw_ref, cb_ref, wa_ref,
                      ba_ref, wi_ref, bi_ref, lam_ref, wo_ref, out_ref, ctail_ref, htail_ref,
                      xext_ref, hc_ref):
    t = pl.program_id(1)
    tt = h_ref.shape[0]
    hist = 8

    @pl.when(t == 0)
    def _():
        xext_ref[0:hist, :] = jnp.zeros((hist, D_MODEL), F32)
        hc_ref[...] = jnp.zeros_like(hc_ref)

    x = h_ref[...]
    gate, xb = _rg_branches(x, g_pre_ref[...], wgx_ref)
    xext_ref[hist:hist + tt, :] = xb
    xc = cb_ref[...]
    for tap in range(CONV_WIDTH):
        off = hist - (CONV_WIDTH - 1) + tap
        xc = xc + xext_ref[off:off + tt, :] * cw_ref[tap:tap + 1, :]
    a, mult, gx = _rg_gates(xc, wa_ref, ba_ref[...], wi_ref, bi_ref[...], lam_ref[...])
    row = lax.broadcasted_iota(jnp.int32, (tt, 1), 0)
    mult = jnp.where((row == 0) & (t == 0), 1.0, mult)
    b = mult * gx
    k = 1
    while k < tt:
        keep = row >= k
        a_prev = jnp.where(keep, pltpu.roll(a, k, 0), 1.0)
        b_prev = jnp.where(keep, pltpu.roll(b, k, 0), 0.0)
        b = b + a * b_prev
        a = a * a_prev
        k *= 2
    hs = b + a * hc_ref[...]
    hc_ref[...] = hs[tt - 1:tt, :]
    y = (hs * gate).astype(BF16)
    out_ref[...] = x + _rms(jnp.dot(y, wo_ref[...], preferred_element_type=F32), g_post_ref[...])
    xext_ref[0:hist, :] = xb[tt - hist:, :]

    @pl.when(t == tail_tile)
    def _():
        ctail_ref[...] = xb[tail_off:tail_off + hist, :]
        htail_ref[...] = hs[tail_off:tail_off + hist, :]


def _rg_weight_specs():
    return [_full((1, D_MODEL)), _full((1, D_MODEL)), _full((D_MODEL, 2 * D_MODEL)),
            _full((CONV_WIDTH, D_MODEL)), _full((1, D_MODEL)),
            _full((RG_HEADS, RG_BLOCK, RG_BLOCK)), _full((1, D_MODEL)),
            _full((RG_HEADS, RG_BLOCK, RG_BLOCK)), _full((1, D_MODEL)),
            _full((1, D_MODEL)), _full((D_MODEL, D_MODEL))]


def _rg_prompt_call(h, weights, batch, t_pad, t_real):
    tt = TIME_TILE
    nt = t_pad // tt
    hist = 8
    tail_start = t_real - hist
    kern = functools.partial(_rg_prompt_kernel, tail_start // tt, tail_start % tt)
    tail = pl.BlockSpec((None, hist, D_MODEL), lambda b, t: (b, 0, 0))
    return pl.pallas_call(
        kern,
        grid=(batch, nt),
        in_specs=[pl.BlockSpec((tt, D_MODEL), lambda b, t: (b * nt + t, 0))] + _rg_weight_specs(),
        out_specs=[pl.BlockSpec((tt, D_MODEL), lambda b, t: (b * nt + t, 0)), tail, tail],
        out_shape=[jax.ShapeDtypeStruct((batch * t_pad, D_MODEL), F32),
                   jax.ShapeDtypeStruct((batch, hist, D_MODEL), F32),
                   jax.ShapeDtypeStruct((batch, hist, D_MODEL), F32)],
        scratch_shapes=[pltpu.VMEM((hist + tt, D_MODEL), F32), pltpu.VMEM((1, D_MODEL), F32)],
        compiler_params=_params(2),
        name="rglru_prompt",
    )(h, *weights)


def _rg_sample_kernel(h_ref, g_pre_ref, g_post_ref, wgx_ref, cw_ref, cb_ref, wa_ref, ba_ref, wi_ref, bi_ref,
                      lam_ref, wo_ref, cs_ref, hp_ref, out_ref, xb_ref, hn_ref):
    x = h_ref[...]
    gate, xb = _rg_branches(x, g_pre_ref[...], wgx_ref)
    xb_ref[...] = xb
    xc = cb_ref[...]
    for tap in range(CONV_WIDTH - 1):
        xc = xc + cs_ref[tap] * cw_ref[tap:tap + 1, :]
    xc = xc + xb * cw_ref[CONV_WIDTH - 1:CONV_WIDTH, :]
    a, mult, gx = _rg_gates(xc, wa_ref, ba_ref[...], wi_ref, bi_ref[...], lam_ref[...])
    hn = a * hp_ref[...] + mult * gx
    hn_ref[...] = hn
    y = (hn * gate).astype(BF16)
    out_ref[...] = x + _rms(jnp.dot(y, wo_ref[...], preferred_element_type=F32), g_post_ref[...])


def _rg_sample_call(h, weights, conv_state_t, h_prev):
    n = h.shape[0]
    out = jax.ShapeDtypeStruct((n, D_MODEL), F32)
    return pl.pallas_call(
        _rg_sample_kernel,
        out_shape=[out, out, out],
        compiler_params=pltpu.CompilerParams(vmem_limit_bytes=VMEM_LIMIT),
        name="rglru_sample",
    )(h, *weights, conv_state_t, h_prev)


def kernel(x_prompt, x_sample, cache_k, cache_v, page_table, state_pool, state_conv, state_h, meta_tokens, norm_g, attn_w_qkv, attn_w_o, attn_bias, pool_w, pool_scale, rg_w_gate, rg_w_x, rg_conv_w, rg_conv_b, rg_w_a, rg_b_a, rg_w_i, rg_b_i, rg_lambda, rg_w_out, mlp_w_up, mlp_w_down):
    batch, seq, _ = x_prompt.shape
    n_seq = x_sample.shape[0]
    t_real = seq + N_META
    t_unit = math.lcm(Q_BLOCK, TIME_TILE)
    t_pad = -(-t_real // t_unit) * t_unit
    assert (batch * t_pad) % ROW_TILE == 0 and x_sample.shape[1] == 1
    n_phys = cache_k.shape[1]

    meta = jnp.broadcast_to(meta_tokens[None], (batch, N_META, D_MODEL))
    pad = jnp.zeros((batch, t_pad - t_real, D_MODEL), F32)
    h_p = jnp.concatenate([meta, x_prompt, pad], axis=1).reshape(batch * t_pad, D_MODEL)
    h_s = x_sample.reshape(n_seq, D_MODEL)
    cache_k2 = cache_k.reshape(-1, PAGE_SIZE * N_HEADS, HEAD_DIM)
    cache_v2 = cache_v.reshape(-1, PAGE_SIZE * N_HEADS, HEAD_DIM)
    wneg_p = _suffix_matrix(K_CHUNK)
    wneg_s = _suffix_matrix(PAGE_SIZE)
    row = lambda v: v.reshape(1, D_MODEL)

    k_p, v_p, k_s, v_s = [], [], [], []
    pool_p, pool_s, conv_p, conv_s, hl_p, hl_s = [], [], [], [], [], []
    for layer in range(DEPTH):
        kind, j = layer % N_MIXERS, layer // N_MIXERS
        g = norm_g[layer]
        if kind == 0:
            w_qkv = attn_w_qkv[j].astype(BF16)
            w_o = attn_w_o[j].astype(BF16)
            kv, qkvb = _qkv_call(h_p, row(g[0]), w_qkv)
            bias2 = attn_bias[j] * LOG2E
            o = _attn_prompt_call(qkvb, bias2, wneg_p, batch, t_pad)
            h_p = _proj_res_call(h_p, o, w_o, row(g[1]))
            kv3 = kv.reshape(batch, t_pad, 2, N_HEADS, HEAD_DIM)[:, :t_real]
            k_p.append(kv3[:, :, 0])
            v_p.append(kv3[:, :, 1])

            kv, qkvb = _qkv_call(h_s, row(g[0]), w_qkv)
            bias_t = jnp.broadcast_to(bias2[:, None, None], (N_HEADS, SUBLANES, PAGE_SIZE))
            q_s = jnp.broadcast_to(qkvb[:, :D_MODEL].reshape(n_seq, N_HEADS, 1, HEAD_DIM),
                                   (n_seq, N_HEADS, SUBLANES, HEAD_DIM))
            o = _attn_sample_call(page_table, q_s, cache_k2, cache_v2, j * n_phys, bias_t, wneg_s)
            h_s = _proj_res_call(h_s, o[:, :, 0].reshape(n_seq, D_MODEL), w_o, row(g[1]))
            kv3 = kv.reshape(n_seq, 1, 2, N_HEADS, HEAD_DIM)
            k_s.append(kv3[:, :, 0])
            v_s.append(kv3[:, :, 1])
        elif kind == 1:
            wp = pool_w[j].astype(BF16)
            ps = row(pool_scale[j])
            h_p, tail = _pool_prompt_call(h_p, row(g[0]), row(g[1]), wp, ps, batch, t_pad, t_real)
            pool_p.append(tail[:, 1:])
            st = state_pool[j]
            h_s, u_s = _pool_sample_call(h_s, row(g[0]), row(g[1]), wp, ps, jnp.swapaxes(st, 0, 1))
            pool_s.append(jnp.concatenate([st[:, 1:], u_s[:, None]], axis=1))
        else:
            weights = (row(g[0]), row(g[1]),
                       jnp.concatenate([rg_w_gate[j], rg_w_x[j]], axis=1).astype(BF16),
                       rg_conv_w[j], row(rg_conv_b[j]),
                       rg_w_a[j].astype(BF16), row(rg_b_a[j]), rg_w_i[j].astype(BF16), row(rg_b_i[j]),
                       row(rg_lambda[j]), rg_w_out[j].astype(BF16))
            h_p, ctail, htail = _rg_prompt_call(h_p, weights, batch, t_pad, t_real)
            conv_p.append(ctail[:, 8 - (CONV_WIDTH - 1):])
            hl_p.append(htail[:, 7])
            cs = state_conv[j]
            h_s, xb_s, hn_s = _rg_sample_call(h_s, weights, jnp.swapaxes(cs, 0, 1), state_h[j])
            conv_s.append(jnp.concatenate([cs[:, 1:], xb_s[:, None]], axis=1))
            hl_s.append(hn_s)
        w_up = mlp_w_up[layer].astype(BF16)
        w_down = mlp_w_down[layer].astype(BF16)
        h_p = _mlp_call(h_p, row(g[2]), row(g[3]), w_up, w_down)
        h_s = _mlp_call(h_s, row(g[2]), row(g[3]), w_up, w_down)

    y_prompt = h_p.reshape(batch, t_pad, D_MODEL)[:, N_META:t_real]
    y_sample = h_s.reshape(n_seq, 1, D_MODEL)
    return (y_prompt, y_sample, jnp.stack(k_p), jnp.stack(v_p), jnp.stack(k_s), jnp.stack(v_s),
            jnp.stack(pool_p), jnp.stack(pool_s), jnp.stack(conv_p), jnp.stack(conv_s),
            jnp.stack(hl_p), jnp.stack(hl_s))
```

```python
import functools
import math

import jax
import jax.numpy as jnp
from jax import lax
from jax.experimental import pallas as pl
from jax.experimental.pallas import tpu as pltpu

F32 = jnp.float32
BF16 = jnp.bfloat16

D_MODEL = 1024
N_HEADS = 16
HEAD_DIM = D_MODEL // N_HEADS
N_PAIRS = N_HEADS // 2
D_FF = 4 * D_MODEL
N_META = 16
N_MIXERS = 3
DEPTH = 4
PAGE_SIZE = 128
POOL_WINDOWS = (2, 4, 8, 16)
POOL_GROUP = D_MODEL // len(POOL_WINDOWS)
POOL_BUF = max(POOL_WINDOWS) - 1
RG_HEADS = 4
RG_BLOCK = D_MODEL // RG_HEADS
CONV_WIDTH = 4
RG_C = 8.0
RMS_EPS = 1e-6
LOG2E = math.log2(math.e)

LANES = 128
SUBLANES = 8
Q_BLOCK = 256
K_CHUNK = 256
PAGES_PER_STEP = 8
ROW_TILE = 512
TIME_TILE = 256
FF_CHUNK = 1024
VMEM_LIMIT = 56 * 1024 * 1024


def _params(n_axes, vmem=VMEM_LIMIT):
    return pltpu.CompilerParams(dimension_semantics=("arbitrary",) * n_axes, vmem_limit_bytes=vmem)


def _rms(x, g):
    ms = jnp.mean(x * x, axis=-1, keepdims=True)
    return x * lax.rsqrt(ms + RMS_EPS) * g


def _softplus(x):
    return jnp.maximum(x, 0.0) + jnp.log1p(jnp.exp(-jnp.abs(x)))


def _full(shape):
    nd = len(shape)
    return pl.BlockSpec(shape, lambda *_: (0,) * nd)


def _qkv_kernel(x_ref, g_ref, w_ref, kv_ref, qkvb_ref):
    u = _rms(x_ref[...], g_ref[...]).astype(BF16)
    y = jnp.dot(u, w_ref[...], preferred_element_type=F32)
    kv_ref[...] = y[:, D_MODEL:]
    qkvb_ref[:, :D_MODEL] = (y[:, :D_MODEL] * (LOG2E * HEAD_DIM ** -0.5)).astype(BF16)
    qkvb_ref[:, D_MODEL:2 * D_MODEL] = y[:, D_MODEL:2 * D_MODEL].astype(BF16)
    v = y[:, 2 * D_MODEL:]
    even = (lax.broadcasted_iota(jnp.int32, v.shape, 1) & HEAD_DIM) == 0
    qkvb_ref[:, 2 * D_MODEL:3 * D_MODEL] = jnp.where(even, v, 0.0).astype(BF16)
    qkvb_ref[:, 3 * D_MODEL:] = jnp.where(even, 0.0, v).astype(BF16)


def _qkv_call(h, g, w):
    n = h.shape[0]
    tm = min(ROW_TILE, n)
    return pl.pallas_call(
        _qkv_kernel,
        grid=(n // tm,),
        in_specs=[pl.BlockSpec((tm, D_MODEL), lambda i: (i, 0)), _full((1, D_MODEL)),
                  _full((D_MODEL, 3 * D_MODEL))],
        out_specs=[pl.BlockSpec((tm, 2 * D_MODEL), lambda i: (i, 0)),
                   pl.BlockSpec((tm, 4 * D_MODEL), lambda i: (i, 0))],
        out_shape=[jax.ShapeDtypeStruct((n, 2 * D_MODEL), F32),
                   jax.ShapeDtypeStruct((n, 4 * D_MODEL), BF16)],
        compiler_params=_params(1),
        name="qkv_proj",
    )(h, g, w)


def _suffix_matrix(n):
    j = jnp.arange(n)[:, None]
    s = jnp.arange(n)[None, :]
    return jnp.where(j > s, -1.0, 0.0).astype(BF16)


def _stick_prep(z2, mask):
    sp = jnp.maximum(z2, 0.0) + jnp.log(1.0 + jnp.exp2(-jnp.abs(z2))) * LOG2E
    logb = z2 - sp
    return logb, sp if mask is None else jnp.where(mask, sp, 0.0)


def _stick_weights(logb, sp, wneg, carry, mask):
    w = jnp.exp2(logb + jnp.dot(sp, wneg, preferred_element_type=F32) + carry)
    return w if mask is None else jnp.where(mask, w, 0.0)


def _attn_prompt_kernel(bias_ref, q_ref, k_ref, v0_ref, v1_ref, wneg_ref, o_ref,
                        logb_ref, sp_ref, tot_ref, w_ref, carry_ref, acc_ref):
    p = pl.program_id(1)
    qb = pl.program_id(2)
    m = 2 * Q_BLOCK
    lane = lax.broadcasted_iota(jnp.int32, (Q_BLOCK, LANES), 1)
    q = q_ref[...].astype(F32)
    q2 = jnp.concatenate([jnp.where(lane < HEAD_DIM, q, 0.0), jnp.where(lane >= HEAD_DIM, q, 0.0)],
                         axis=0).astype(BF16)
    row = lax.broadcasted_iota(jnp.int32, (m, K_CHUNK), 0)
    col = lax.broadcasted_iota(jnp.int32, (m, K_CHUNK), 1)
    bias_t = jnp.where(row < Q_BLOCK, bias_ref[2 * p], bias_ref[2 * p + 1])
    wneg = wneg_ref[...]

    def chunk(ref, kb):
        return ref[pl.ds(pl.multiple_of(jnp.maximum(kb, 0) * K_CHUNK, K_CHUNK), K_CHUNK), :]

    def logits(kb, mask):
        z = lax.dot_general(q2, chunk(k_ref, kb), (((1,), (1,)), ((), ())), preferred_element_type=F32)
        logb_ref[...], sp = _stick_prep(z + bias_t, mask)
        sp_ref[...] = sp.astype(BF16)
        tot_ref[...] = sp[:, :LANES] + sp[:, LANES:]

    def weights(mask):
        carry = carry_ref[...]
        w_ref[...] = _stick_weights(logb_ref[...], sp_ref[...], wneg, carry, mask).astype(BF16)
        carry_ref[...] = carry - jnp.sum(tot_ref[...], axis=1, keepdims=True)

    def weighted_values(kb):
        w2 = jnp.concatenate([w_ref[:Q_BLOCK, :], w_ref[Q_BLOCK:, :]], axis=1)
        v2 = jnp.concatenate([chunk(v0_ref, kb), chunk(v1_ref, kb)], axis=0)
        acc_ref[...] += jnp.dot(w2, v2, preferred_element_type=F32)

    diag_mask = col < (row & (Q_BLOCK - 1))
    carry_ref[...] = jnp.zeros_like(carry_ref)
    acc_ref[...] = jnp.zeros_like(acc_ref)
    logits(qb, diag_mask)
    weights(diag_mask)
    logits(qb - 1, None)

    def trip(kb):
        weighted_values(kb)
        weights(None)
        logits(kb - 2, None)

    def two_trips(i, _):
        trip(qb - 2 * i)
        trip(qb - 2 * i - 1)
        return 0

    n_trips = qb + 1
    lax.fori_loop(0, n_trips // 2, two_trips, 0)

    @pl.when(n_trips % 2 == 1)
    def _():
        trip(0)

    o_ref[...] = acc_ref[...].astype(BF16)


def _attn_prompt_call(qkvb, bias, wneg, batch, t_pad):
    assert Q_BLOCK == K_CHUNK == 2 * LANES and t_pad % Q_BLOCK == 0
    nqb = t_pad // Q_BLOCK
    m = 2 * Q_BLOCK

    def cols(section):
        return pl.BlockSpec((t_pad, LANES), lambda b, p, i: (b, section * N_PAIRS + p))

    return pl.pallas_call(
        _attn_prompt_kernel,
        grid=(batch, N_PAIRS, nqb),
        in_specs=[pl.BlockSpec(memory_space=pltpu.SMEM),
                  pl.BlockSpec((Q_BLOCK, LANES), lambda b, p, i: (b * nqb + i, p)),
                  cols(1), cols(2), cols(3), _full((K_CHUNK, K_CHUNK))],
        out_specs=pl.BlockSpec((Q_BLOCK, LANES), lambda b, p, i: (b * nqb + i, p)),
        out_shape=jax.ShapeDtypeStruct((batch * t_pad, D_MODEL), BF16),
        scratch_shapes=[pltpu.VMEM((m, K_CHUNK), F32), pltpu.VMEM((m, K_CHUNK), BF16), pltpu.VMEM((m, LANES), F32),
                        pltpu.VMEM((m, K_CHUNK), BF16), pltpu.VMEM((m, 1), F32),
                        pltpu.VMEM((Q_BLOCK, LANES), F32)],
        compiler_params=_params(3),
        name="attn_prompt",
    )(bias, qkvb, qkvb, qkvb, qkvb, wneg)


def _attn_sample_kernel(pt_ref, q_ref, bias_ref, wneg_ref, *refs):
    k_refs = refs[:PAGES_PER_STEP]
    v_refs = refs[PAGES_PER_STEP:2 * PAGES_PER_STEP]
    o_ref, acc_ref, carry_ref = refs[2 * PAGES_PER_STEP:]
    step = pl.program_id(1)

    @pl.when(step == 0)
    def _():
        acc_ref[...] = jnp.zeros_like(acc_ref)
        carry_ref[...] = jnp.zeros_like(carry_ref)

    q = q_ref[...]
    rows = N_HEADS * SUBLANES
    z = jnp.concatenate(
        [(jnp.einsum("hqd,hdk->hqk", q, k_ref[...].astype(BF16), preferred_element_type=F32)
          + bias_ref[...]).reshape(rows, PAGE_SIZE) for k_ref in k_refs], axis=0)
    logb, sp = _stick_prep(z, None)
    carry = carry_ref[...]
    carries = []
    for j in range(PAGES_PER_STEP):
        carries.append(carry)
        carry = carry - jnp.sum(sp[j * rows:(j + 1) * rows], axis=1, keepdims=True)
    carry_ref[...] = carry
    w = _stick_weights(logb, sp.astype(BF16), wneg_ref[...], jnp.concatenate(carries, axis=0), None).astype(BF16)
    acc = acc_ref[...]
    for j, v_ref in enumerate(v_refs):
        acc = acc + jnp.einsum("hqk,hdk->hqd", w[j * rows:(j + 1) * rows].reshape(N_HEADS, SUBLANES, PAGE_SIZE),
                               v_ref[...].astype(BF16), preferred_element_type=F32)
    acc_ref[...] = acc

    @pl.when(step == pl.num_programs(1) - 1)
    def _():
        o_ref[...] = acc.astype(BF16)


def _attn_sample_call(page_table, q, cache_k, cache_v, page_base, bias_t, wneg):
    n_seq, n_pages = page_table.shape
    assert n_pages % PAGES_PER_STEP == 0

    def page(j):
        return lambda b, s, pt: (page_base + pt[b, n_pages - 1 - (s * PAGES_PER_STEP + j)], 0, 0, 0)

    page_specs = [pl.BlockSpec((None, N_HEADS, HEAD_DIM, PAGE_SIZE), page(j)) for j in range(PAGES_PER_STEP)]
    q_spec = pl.BlockSpec((None, N_HEADS, SUBLANES, HEAD_DIM), lambda b, s, pt: (b, 0, 0, 0))
    return pl.pallas_call(
        _attn_sample_kernel,
        grid_spec=pltpu.PrefetchScalarGridSpec(
            num_scalar_prefetch=1,
            grid=(n_seq, n_pages // PAGES_PER_STEP),
            in_specs=[q_spec,
                      pl.BlockSpec((N_HEADS, SUBLANES, PAGE_SIZE), lambda b, s, pt: (0, 0, 0)),
                      pl.BlockSpec((PAGE_SIZE, PAGE_SIZE), lambda b, s, pt: (0, 0))]
                     + page_specs + page_specs,
            out_specs=q_spec,
            scratch_shapes=[pltpu.VMEM((N_HEADS, SUBLANES, HEAD_DIM), F32),
                            pltpu.VMEM((N_HEADS * SUBLANES, 1), F32)]),
        out_shape=jax.ShapeDtypeStruct((n_seq, N_HEADS, SUBLANES, HEAD_DIM), BF16),
        compiler_params=_params(2),
        name="attn_sample",
    )(page_table, q, bias_t, wneg, *([cache_k] * PAGES_PER_STEP), *([cache_v] * PAGES_PER_STEP))


def _proj_res_kernel(h_ref, o_ref, w_ref, g_ref, out_ref):
    m = jnp.dot(o_ref[...], w_ref[...], preferred_element_type=F32)
    out_ref[...] = h_ref[...] + _rms(m, g_ref[...])


def _proj_res_call(h, o, w, g):
    n = h.shape[0]
    tm = min(ROW_TILE, n)
    row = pl.BlockSpec((tm, D_MODEL), lambda i: (i, 0))
    return pl.pallas_call(
        _proj_res_kernel,
        grid=(n // tm,),
        in_specs=[row, row, _full((D_MODEL, D_MODEL)), _full((1, D_MODEL))],
        out_specs=row,
        out_shape=jax.ShapeDtypeStruct((n, D_MODEL), F32),
        compiler_params=_params(1),
        name="attn_out_proj",
    )(h, o, w, g)


def _mlp_kernel(h_ref, g_pre_ref, g_post_ref, wu_ref, wd_ref, out_ref):
    x = h_ref[...]
    u = _rms(x, g_pre_ref[...]).astype(BF16)
    f = jnp.zeros(x.shape, F32)
    for c in range(D_FF // FF_CHUNK):
        sl = slice(c * FF_CHUNK, (c + 1) * FF_CHUNK)
        a = jnp.maximum(jnp.dot(u, wu_ref[:, sl], preferred_element_type=F32), 0.0)
        f = f + jnp.dot((a * a).astype(BF16), wd_ref[sl, :], preferred_element_type=F32)
    out_ref[...] = x + _rms(f, g_post_ref[...])


def _mlp_call(h, g_pre, g_post, wu, wd):
    n = h.shape[0]
    tm = min(ROW_TILE, n)
    row = pl.BlockSpec((tm, D_MODEL), lambda i: (i, 0))
    return pl.pallas_call(
        _mlp_kernel,
        grid=(n // tm,),
        in_specs=[row, _full((1, D_MODEL)), _full((1, D_MODEL)),
                  pl.BlockSpec((D_MODEL, D_FF), lambda i: (0, 0), pipeline_mode=pl.Buffered(1)),
                  pl.BlockSpec((D_FF, D_MODEL), lambda i: (0, 0), pipeline_mode=pl.Buffered(1))],
        out_specs=row,
        out_shape=jax.ShapeDtypeStruct((n, D_MODEL), F32),
        compiler_params=_params(1),
        name="mlp",
    )(h, g_pre, g_post, wu, wd)


def _pool_mix(u, shifted, cnt_of, wp_ref, ps):
    groups = []
    for gi, w in enumerate(POOL_WINDOWS):
        sl = slice(gi * POOL_GROUP, (gi + 1) * POOL_GROUP)
        ug = u[:, sl]
        win = ug
        for j in range(1, w):
            win = win + shifted(j, sl)
        d = win / cnt_of(w) - ug
        groups.append(jnp.dot(d.astype(BF16), wp_ref[gi], preferred_element_type=F32))
    return jnp.concatenate(groups, axis=1) * ps


def _pool_prompt_kernel(tail_tile, tail_off, h_ref, g_pre_ref, g_post_ref, wp_ref, ps_ref,
                        out_ref, tail_ref, ext_ref):
    t = pl.program_id(1)
    tt = h_ref.shape[0]

    @pl.when(t == 0)
    def _():
        ext_ref[0:POOL_BUF + 1, :] = jnp.zeros((POOL_BUF + 1, D_MODEL), F32)

    x = h_ref[...]
    u = _rms(x, g_pre_ref[...])
    hist = POOL_BUF + 1
    ext_ref[hist:hist + tt, :] = u
    pos = t * tt + lax.broadcasted_iota(jnp.int32, (tt, 1), 0)
    m = _pool_mix(u, lambda j, sl: ext_ref[hist - j:hist - j + tt, sl],
                  lambda w: jnp.minimum(pos + 1, w).astype(F32), wp_ref, ps_ref[...])
    out_ref[...] = x + _rms(m, g_post_ref[...])
    ext_ref[0:hist, :] = u[tt - hist:, :]

    @pl.when(t == tail_tile)
    def _():
        tail_ref[...] = u[tail_off:tail_off + hist, :]


def _pool_prompt_call(h, g_pre, g_post, wp, ps, batch, t_pad, t_real):
    tt = TIME_TILE
    nt = t_pad // tt
    hist = POOL_BUF + 1
    tail_start = t_real - hist
    kern = functools.partial(_pool_prompt_kernel, tail_start // tt, tail_start % tt)
    return pl.pallas_call(
        kern,
        grid=(batch, nt),
        in_specs=[pl.BlockSpec((tt, D_MODEL), lambda b, t: (b * nt + t, 0)),
                  _full((1, D_MODEL)), _full((1, D_MODEL)),
                  _full((len(POOL_WINDOWS), POOL_GROUP, POOL_GROUP)), _full((1, D_MODEL))],
        out_specs=[pl.BlockSpec((tt, D_MODEL), lambda b, t: (b * nt + t, 0)),
                   pl.BlockSpec((None, hist, D_MODEL), lambda b, t: (b, 0, 0))],
        out_shape=[jax.ShapeDtypeStruct((batch * t_pad, D_MODEL), F32),
                   jax.ShapeDtypeStruct((batch, hist, D_MODEL), F32)],
        scratch_shapes=[pltpu.VMEM((hist + tt, D_MODEL), F32)],
        compiler_params=_params(2),
        name="pool_prompt",
    )(h, g_pre, g_post, wp, ps)


def _pool_sample_kernel(h_ref, g_pre_ref, g_post_ref, wp_ref, ps_ref, st_ref, out_ref, u_ref):
    x = h_ref[...]
    u = _rms(x, g_pre_ref[...])
    u_ref[...] = u
    m = _pool_mix(u, lambda j, sl: st_ref[POOL_BUF - j, :, sl], lambda w: float(w), wp_ref, ps_ref[...])
    out_ref[...] = x + _rms(m, g_post_ref[...])


def _pool_sample_call(h, g_pre, g_post, wp, ps, state_t):
    n = h.shape[0]
    return pl.pallas_call(
        _pool_sample_kernel,
        out_shape=[jax.ShapeDtypeStruct((n, D_MODEL), F32), jax.ShapeDtypeStruct((n, D_MODEL), F32)],
        compiler_params=pltpu.CompilerParams(vmem_limit_bytes=VMEM_LIMIT),
        name="pool_sample",
    )(h, g_pre, g_post, wp, ps, state_t)


def _gelu_tanh(x):
    return 0.5 * x * (1.0 + jnp.tanh(math.sqrt(2.0 / math.pi) * (x + 0.044715 * (x * x * x))))


def _rg_branches(x, g_pre, wgx_ref):
    u = _rms(x, g_pre).astype(BF16)
    gx = jnp.dot(u, wgx_ref[...], preferred_element_type=F32)
    return _gelu_tanh(gx[:, :D_MODEL]), gx[:, D_MODEL:]


def _rg_gates(xc, wa_ref, ba, wi_ref, bi, lam):
    xcb = xc.astype(BF16)
    ra, ri = [], []
    for hd in range(RG_HEADS):
        sl = slice(hd * RG_BLOCK, (hd + 1) * RG_BLOCK)
        ra.append(jnp.dot(xcb[:, sl], wa_ref[hd], preferred_element_type=F32))
        ri.append(jnp.dot(xcb[:, sl], wi_ref[hd], preferred_element_type=F32))
    r = jax.nn.sigmoid(jnp.concatenate(ra, axis=1) + ba)
    ig = jax.nn.sigmoid(jnp.concatenate(ri, axis=1) + bi)
    log_a = -RG_C * r * _softplus(-lam)
    a = jnp.exp(log_a)
    mult = jnp.sqrt(-jnp.tanh(log_a) * (a * a + 1.0))
    return a, mult, ig * xc


def _rg_prompt_kernel(tail_tile, tail_off, h_ref, g_pre_ref, g_post_ref, wgx_ref, cw_ref, cb_ref, wa_ref,
                      ba_ref, wi_ref, bi_ref, lam_ref, wo_ref, out_ref, ctail_ref, htail_ref,
                      xext_ref, hc_ref):
    t = pl.program_id(1)
    tt = h_ref.shape[0]
    hist = 8

    @pl.when(t == 0)
    def _():
        xext_ref[0:hist, :] = jnp.zeros((hist, D_MODEL), F32)
        hc_ref[...] = jnp.zeros_like(hc_ref)

    x = h_ref[...]
    gate, xb = _rg_branches(x, g_pre_ref[...], wgx_ref)
    xext_ref[hist:hist + tt, :] = xb
    xc = cb_ref[...]
    for tap in range(CONV_WIDTH):
        off = hist - (CONV_WIDTH - 1) + tap
        xc = xc + xext_ref[off:off + tt, :] * cw_ref[tap:tap + 1, :]
    a, mult, gx = _rg_gates(xc, wa_ref, ba_ref[...], wi_ref, bi_ref[...], lam_ref[...])
    row = lax.broadcasted_iota(jnp.int32, (tt, 1), 0)
    mult = jnp.where((row == 0) & (t == 0), 1.0, mult)
    b = mult * gx
    k = 1
    while k < tt:
        keep = row >= k
        a_prev = jnp.where(keep, pltpu.roll(a, k, 0), 1.0)
        b_prev = jnp.where(keep, pltpu.roll(b, k, 0), 0.0)
        b = b + a * b_prev
        a = a * a_prev
        k *= 2
    hs = b + a * hc_ref[...]
    hc_ref[...] = hs[tt - 1:tt, :]
    y = (hs * gate).astype(BF16)
    out_ref[...] = x + _rms(jnp.dot(y, wo_ref[...], preferred_element_type=F32), g_post_ref[...])
    xext_ref[0:hist, :] = xb[tt - hist:, :]

    @pl.when(t == tail_tile)
    def _():
        ctail_ref[...] = xb[tail_off:tail_off + hist, :]
        htail_ref[...] = hs[tail_off:tail_off + hist, :]


def _rg_weight_specs():
    return [_full((1, D_MODEL)), _full((1, D_MODEL)), _full((D_MODEL, 2 * D_MODEL)),
            _full((CONV_WIDTH, D_MODEL)), _full((1, D_MODEL)),
            _full((RG_HEADS, RG_BLOCK, RG_BLOCK)), _full((1, D_MODEL)),
            _full((RG_HEADS, RG_BLOCK, RG_BLOCK)), _full((1, D_MODEL)),
            _full((1, D_MODEL)), _full((D_MODEL, D_MODEL))]


def _rg_prompt_call(h, weights, batch, t_pad, t_real):
    tt = TIME_TILE
    nt = t_pad // tt
    hist = 8
    tail_start = t_real - hist
    kern = functools.partial(_rg_prompt_kernel, tail_start // tt, tail_start % tt)
    tail = pl.BlockSpec((None, hist, D_MODEL), lambda b, t: (b, 0, 0))
    return pl.pallas_call(
        kern,
        grid=(batch, nt),
        in_specs=[pl.BlockSpec((tt, D_MODEL), lambda b, t: (b * nt + t, 0))] + _rg_weight_specs(),
        out_specs=[pl.BlockSpec((tt, D_MODEL), lambda b, t: (b * nt + t, 0)), tail, tail],
        out_shape=[jax.ShapeDtypeStruct((batch * t_pad, D_MODEL), F32),
                   jax.ShapeDtypeStruct((batch, hist, D_MODEL), F32),
                   jax.ShapeDtypeStruct((batch, hist, D_MODEL), F32)],
        scratch_shapes=[pltpu.VMEM((hist + tt, D_MODEL), F32), pltpu.VMEM((1, D_MODEL), F32)],
        compiler_params=_params(2),
        name="rglru_prompt",
    )(h, *weights)


def _rg_sample_kernel(h_ref, g_pre_ref, g_post_ref, wgx_ref, cw_ref, cb_ref, wa_ref, ba_ref, wi_ref, bi_ref,
                      lam_ref, wo_ref, cs_ref, hp_ref, out_ref, xb_ref, hn_ref):
    x = h_ref[...]
    gate, xb = _rg_branches(x, g_pre_ref[...], wgx_ref)
    xb_ref[...] = xb
    xc = cb_ref[...]
    for tap in range(CONV_WIDTH - 1):
        xc = xc + cs_ref[tap] * cw_ref[tap:tap + 1, :]
    xc = xc + xb * cw_ref[CONV_WIDTH - 1:CONV_WIDTH, :]
    a, mult, gx = _rg_gates(xc, wa_ref, ba_ref[...], wi_ref, bi_ref[...], lam_ref[...])
    hn = a * hp_ref[...] + mult * gx
    hn_ref[...] = hn
    y = (hn * gate).astype(BF16)
    out_ref[...] = x + _rms(jnp.dot(y, wo_ref[...], preferred_element_type=F32), g_post_ref[...])


def _rg_sample_call(h, weights, conv_state_t, h_prev):
    n = h.shape[0]
    out = jax.ShapeDtypeStruct((n, D_MODEL), F32)
    return pl.pallas_call(
        _rg_sample_kernel,
        out_shape=[out, out, out],
        compiler_params=pltpu.CompilerParams(vmem_limit_bytes=VMEM_LIMIT),
        name="rglru_sample",
    )(h, *weights, conv_state_t, h_prev)


def kernel(x_prompt, x_sample, cache_k, cache_v, page_table, state_pool, state_conv, state_h, meta_tokens, norm_g, attn_w_qkv, attn_w_o, attn_bias, pool_w, pool_scale, rg_w_gate, rg_w_x, rg_conv_w, rg_conv_b, rg_w_a, rg_b_a, rg_w_i, rg_b_i, rg_lambda, rg_w_out, mlp_w_up, mlp_w_down):
    batch, seq, _ = x_prompt.shape
    n_seq = x_sample.shape[0]
    t_real = seq + N_META
    t_unit = math.lcm(Q_BLOCK, TIME_TILE)
    t_pad = -(-t_real // t_unit) * t_unit
    assert (batch * t_pad) % ROW_TILE == 0 and x_sample.shape[1] == 1
    n_phys = cache_k.shape[1]

    meta = jnp.broadcast_to(meta_tokens[None], (batch, N_META, D_MODEL))
    pad = jnp.zeros((batch, t_pad - t_real, D_MODEL), F32)
    h_p = jnp.concatenate([meta, x_prompt, pad], axis=1).reshape(batch * t_pad, D_MODEL)
    h_s = x_sample.reshape(n_seq, D_MODEL)
    cache_k2 = cache_k.transpose(0, 1, 3, 4, 2).reshape(-1, N_HEADS, HEAD_DIM, PAGE_SIZE)
    cache_v2 = cache_v.transpose(0, 1, 3, 4, 2).reshape(-1, N_HEADS, HEAD_DIM, PAGE_SIZE)
    wneg_p = _suffix_matrix(K_CHUNK)
    wneg_s = _suffix_matrix(PAGE_SIZE)
    row = lambda v: v.reshape(1, D_MODEL)

    k_p, v_p, k_s, v_s = [], [], [], []
    pool_p, pool_s, conv_p, conv_s, hl_p, hl_s = [], [], [], [], [], []
    for layer in range(DEPTH):
        kind, j = layer % N_MIXERS, layer // N_MIXERS
        g = norm_g[layer]
        if kind == 0:
            w_qkv = attn_w_qkv[j].astype(BF16)
            w_o = attn_w_o[j].astype(BF16)
            kv, qkvb = _qkv_call(h_p, row(g[0]), w_qkv)
            bias2 = attn_bias[j] * LOG2E
            o = _attn_prompt_call(qkvb, bias2, wneg_p, batch, t_pad)
            h_p = _proj_res_call(h_p, o, w_o, row(g[1]))
            kv3 = kv.reshape(batch, t_pad, 2, N_HEADS, HEAD_DIM)[:, :t_real]
            k_p.append(kv3[:, :, 0])
            v_p.append(kv3[:, :, 1])

            kv, qkvb = _qkv_call(h_s, row(g[0]), w_qkv)
            bias_t = jnp.broadcast_to(bias2[:, None, None], (N_HEADS, SUBLANES, PAGE_SIZE))
            q_s = jnp.broadcast_to(qkvb[:, :D_MODEL].reshape(n_seq, N_HEADS, 1, HEAD_DIM),
                                   (n_seq, N_HEADS, SUBLANES, HEAD_DIM))
            o = _attn_sample_call(page_table, q_s, cache_k2, cache_v2, j * n_phys, bias_t, wneg_s)
            h_s = _proj_res_call(h_s, o[:, :, 0].reshape(n_seq, D_MODEL), w_o, row(g[1]))
            kv3 = kv.reshape(n_seq, 1, 2, N_HEADS, HEAD_DIM)
            k_s.append(kv3[:, :, 0])
            v_s.append(kv3[:, :, 1])
        elif kind == 1:
            wp = pool_w[j].astype(BF16)
            ps = row(pool_scale[j])
            h_p, tail = _pool_prompt_call(h_p, row(g[0]), row(g[1]), wp, ps, batch, t_pad, t_real)
            pool_p.append(tail[:, 1:])
            st = state_pool[j]
            h_s, u_s = _pool_sample_call(h_s, row(g[0]), row(g[1]), wp, ps, jnp.swapaxes(st, 0, 1))
            pool_s.append(jnp.concatenate([st[:, 1:], u_s[:, None]], axis=1))
        else:
            weights = (row(g[0]), row(g[1]),
                       jnp.concatenate([rg_w_gate[j], rg_w_x[j]], axis=1).astype(BF16),
                       rg_conv_w[j], row(rg_conv_b[j]),
                       rg_w_a[j].astype(BF16), row(rg_b_a[j]), rg_w_i[j].astype(BF16), row(rg_b_i[j]),
                       row(rg_lambda[j]), rg_w_out[j].astype(BF16))
            h_p, ctail, htail = _rg_prompt_call(h_p, weights, batch, t_pad, t_real)
            conv_p.append(ctail[:, 8 - (CONV_WIDTH - 1):])
            hl_p.append(htail[:, 7])
            cs = state_conv[j]
            h_s, xb_s, hn_s = _rg_sample_call(h_s, weights, jnp.swapaxes(cs, 0, 1), state_h[j])
            conv_s.append(jnp.concatenate([cs[:, 1:], xb_s[:, None]], axis=1))
            hl_s.append(hn_s)
        w_up = mlp_w_up[layer].astype(BF16)
        w_down = mlp_w_down[layer].astype(BF16)
        h_p = _mlp_call(h_p, row(g[2]), row(g[3]), w_up, w_down)
        h_s = _mlp_call(h_s, row(g[2]), row(g[3]), w_up, w_down)

    y_prompt = h_p.reshape(batch, t_pad, D_MODEL)[:, N_META:t_real]
    y_sample = h_s.reshape(n_seq, 1, D_MODEL)
    return (y_prompt, y_sample, jnp.stack(k_p), jnp.stack(v_p), jnp.stack(k_s), jnp.stack(v_s),
            jnp.stack(pool_p), jnp.stack(pool_s), jnp.stack(conv_p), jnp.stack(conv_s),
            jnp.stack(hl_p), jnp.stack(hl_s))
```

```python
import functools
import math

import jax
import jax.numpy as jnp
from jax import lax
from jax.experimental import pallas as pl
from jax.experimental.pallas import tpu as pltpu

F32 = jnp.float32
BF16 = jnp.bfloat16

D_MODEL = 1024
N_HEADS = 16
HEAD_DIM = D_MODEL // N_HEADS
N_PAIRS = N_HEADS // 2
D_FF = 4 * D_MODEL
N_META = 16
N_MIXERS = 3
DEPTH = 4
PAGE_SIZE = 128
POOL_WINDOWS = (2, 4, 8, 16)
POOL_GROUP = D_MODEL // len(POOL_WINDOWS)
POOL_BUF = max(POOL_WINDOWS) - 1
RG_HEADS = 4
RG_BLOCK = D_MODEL // RG_HEADS
CONV_WIDTH = 4
RG_C = 8.0
RMS_EPS = 1e-6
LOG2E = math.log2(math.e)

LANES = 128
SUBLANES = 8
Q_BLOCK = 256
K_CHUNK = 256
PAGES_PER_STEP = 8
ROW_TILE = 512
TIME_TILE = 256
FF_CHUNK = 1024
VMEM_LIMIT = 56 * 1024 * 1024


def _params(n_axes, vmem=VMEM_LIMIT):
    return pltpu.CompilerParams(dimension_semantics=("arbitrary",) * n_axes, vmem_limit_bytes=vmem)


def _rms(x, g):
    ms = jnp.mean(x * x, axis=-1, keepdims=True)
    return x * lax.rsqrt(ms + RMS_EPS) * g


def _softplus(x):
    return jnp.maximum(x, 0.0) + jnp.log1p(jnp.exp(-jnp.abs(x)))


def _full(shape):
    nd = len(shape)
    return pl.BlockSpec(shape, lambda *_: (0,) * nd)


def _qkv_kernel(x_ref, g_ref, w_ref, kv_ref, qkvb_ref):
    u = _rms(x_ref[...], g_ref[...]).astype(BF16)
    y = jnp.dot(u, w_ref[...], preferred_element_type=F32)
    kv_ref[...] = y[:, D_MODEL:]
    qkvb_ref[:, :D_MODEL] = (y[:, :D_MODEL] * (LOG2E * HEAD_DIM ** -0.5)).astype(BF16)
    qkvb_ref[:, D_MODEL:2 * D_MODEL] = y[:, D_MODEL:2 * D_MODEL].astype(BF16)
    v = y[:, 2 * D_MODEL:]
    even = (lax.broadcasted_iota(jnp.int32, v.shape, 1) & HEAD_DIM) == 0
    qkvb_ref[:, 2 * D_MODEL:3 * D_MODEL] = jnp.where(even, v, 0.0).astype(BF16)
    qkvb_ref[:, 3 * D_MODEL:] = jnp.where(even, 0.0, v).astype(BF16)


def _qkv_call(h, g, w):
    n = h.shape[0]
    tm = min(ROW_TILE, n)
    return pl.pallas_call(
        _qkv_kernel,
        grid=(n // tm,),
        in_specs=[pl.BlockSpec((tm, D_MODEL), lambda i: (i, 0)), _full((1, D_MODEL)),
                  _full((D_MODEL, 3 * D_MODEL))],
        out_specs=[pl.BlockSpec((tm, 2 * D_MODEL), lambda i: (i, 0)),
                   pl.BlockSpec((tm, 4 * D_MODEL), lambda i: (i, 0))],
        out_shape=[jax.ShapeDtypeStruct((n, 2 * D_MODEL), F32),
                   jax.ShapeDtypeStruct((n, 4 * D_MODEL), BF16)],
        compiler_params=_params(1),
        name="qkv_proj",
    )(h, g, w)


def _suffix_matrix(n):
    j = jnp.arange(n)[:, None]
    s = jnp.arange(n)[None, :]
    return jnp.where(j > s, -1.0, 0.0).astype(BF16)


def _stick_prep(z2, mask):
    sp = jnp.maximum(z2, 0.0) + jnp.log(1.0 + jnp.exp2(-jnp.abs(z2))) * LOG2E
    logb = z2 - sp
    return logb, sp if mask is None else jnp.where(mask, sp, 0.0)


def _stick_weights(logb, sp, wneg, carry, mask):
    w = jnp.exp2(logb + jnp.dot(sp, wneg, preferred_element_type=F32) + carry)
    return w if mask is None else jnp.where(mask, w, 0.0)


def _attn_prompt_kernel(bias_ref, q_ref, k_ref, v0_ref, v1_ref, wneg_ref, ones2_ref, o_ref,
                        q2_ref, logb_ref, sp_ref, tot_ref, w_ref, carry_ref, acc_ref):
    p = pl.program_id(1)
    m = 2 * Q_BLOCK
    lane = lax.broadcasted_iota(jnp.int32, (Q_BLOCK, LANES), 1)
    row = lax.broadcasted_iota(jnp.int32, (m, K_CHUNK), 0)
    col = lax.broadcasted_iota(jnp.int32, (m, K_CHUNK), 1)
    diag_mask = col < (row & (Q_BLOCK - 1))
    wneg = wneg_ref[...]
    ones2 = ones2_ref[...]
    row_m = lax.broadcasted_iota(jnp.int32, (m, LANES), 0)
    lane_m = lax.broadcasted_iota(jnp.int32, (m, LANES), 1)
    bias = jnp.where(row_m < Q_BLOCK, bias_ref[2 * p], bias_ref[2 * p + 1])
    bias_hi = bias.astype(BF16).astype(F32)
    q2_ref[:, LANES:] = jnp.where(lane_m == 0, bias_hi, jnp.where(lane_m == 1, bias - bias_hi, 0.0)).astype(BF16)

    def chunk(ref, kb):
        return ref[pl.ds(pl.multiple_of(jnp.maximum(kb, 0) * K_CHUNK, K_CHUNK), K_CHUNK), :]

    def logits(kb, mask):
        k2 = jnp.concatenate([chunk(k_ref, kb), ones2], axis=1)
        z = lax.dot_general(q2_ref[...], k2, (((1,), (1,)), ((), ())), preferred_element_type=F32)
        logb_ref[...], sp = _stick_prep(z, mask)
        sp_ref[...] = sp.astype(BF16)
        tot_ref[...] = sp[:, :LANES] + sp[:, LANES:]

    def weights(mask):
        carry = carry_ref[...]
        w_ref[...] = _stick_weights(logb_ref[...], sp_ref[...], wneg, carry, mask).astype(BF16)
        carry_ref[...] = carry - jnp.sum(tot_ref[...], axis=1, keepdims=True)

    def weighted_values(kb):
        w2 = jnp.concatenate([w_ref[:Q_BLOCK, :], w_ref[Q_BLOCK:, :]], axis=1)
        v2 = jnp.concatenate([chunk(v0_ref, kb), chunk(v1_ref, kb)], axis=0)
        acc_ref[...] += jnp.dot(w2, v2, preferred_element_type=F32)

    def trip(kb):
        weighted_values(kb)
        weights(None)
        logits(kb - 2, None)

    def query_block(qb, _):
        q = chunk(q_ref, qb).astype(F32)
        q2_ref[:, :LANES] = jnp.concatenate(
            [jnp.where(lane < HEAD_DIM, q, 0.0), jnp.where(lane >= HEAD_DIM, q, 0.0)], axis=0).astype(BF16)
        carry_ref[...] = jnp.zeros_like(carry_ref)
        acc_ref[...] = jnp.zeros_like(acc_ref)
        logits(qb, diag_mask)
        weights(diag_mask)
        logits(qb - 1, None)

        def two_trips(i, _):
            trip(qb - 2 * i)
            trip(qb - 2 * i - 1)
            return 0

        n_trips = qb + 1
        lax.fori_loop(0, n_trips // 2, two_trips, 0)

        @pl.when(n_trips % 2 == 1)
        def _():
            trip(0)

        o_ref[pl.ds(pl.multiple_of(qb * Q_BLOCK, Q_BLOCK), Q_BLOCK), :] = acc_ref[...].astype(BF16)
        return 0

    lax.fori_loop(0, q_ref.shape[0] // Q_BLOCK, query_block, 0)


def _attn_prompt_call(qkvb, bias, wneg, batch, t_pad):
    assert Q_BLOCK == K_CHUNK == 2 * LANES and t_pad % Q_BLOCK == 0
    m = 2 * Q_BLOCK

    def cols(section):
        return pl.BlockSpec((t_pad, LANES), lambda b, p: (b, section * N_PAIRS + p))

    ones2 = jnp.broadcast_to((jnp.arange(LANES) < 2).astype(BF16), (K_CHUNK, LANES))
    return pl.pallas_call(
        _attn_prompt_kernel,
        grid=(batch, N_PAIRS),
        in_specs=[pl.BlockSpec(memory_space=pltpu.SMEM), cols(0), cols(1), cols(2), cols(3),
                  _full((K_CHUNK, K_CHUNK)), _full((K_CHUNK, LANES))],
        out_specs=cols(0),
        out_shape=jax.ShapeDtypeStruct((batch * t_pad, D_MODEL), BF16),
        scratch_shapes=[pltpu.VMEM((m, 2 * LANES), BF16),
                        pltpu.VMEM((m, K_CHUNK), F32), pltpu.VMEM((m, K_CHUNK), BF16), pltpu.VMEM((m, LANES), F32),
                        pltpu.VMEM((m, K_CHUNK), BF16), pltpu.VMEM((m, 1), F32),
                        pltpu.VMEM((Q_BLOCK, LANES), F32)],
        compiler_params=_params(2),
        name="attn_prompt",
    )(bias, qkvb, qkvb, qkvb, qkvb, wneg, ones2)


def _attn_sample_kernel(pt_ref, q_ref, bias_ref, wneg_ref, *refs):
    k_refs = refs[:PAGES_PER_STEP]
    v_refs = refs[PAGES_PER_STEP:2 * PAGES_PER_STEP]
    o_ref, acc_ref, carry_ref = refs[2 * PAGES_PER_STEP:]
    step = pl.program_id(1)

    @pl.when(step == 0)
    def _():
        acc_ref[...] = jnp.zeros_like(acc_ref)
        carry_ref[...] = jnp.zeros_like(carry_ref)

    q = q_ref[...]
    rows = N_HEADS * SUBLANES
    z = jnp.concatenate(
        [(jnp.einsum("hqd,hdk->hqk", q, k_ref[...].astype(BF16), preferred_element_type=F32)
          + bias_ref[...]).reshape(rows, PAGE_SIZE) for k_ref in k_refs], axis=0)
    logb, sp = _stick_prep(z, None)
    carry = carry_ref[...]
    carries = []
    for j in range(PAGES_PER_STEP):
        carries.append(carry)
        carry = carry - jnp.sum(sp[j * rows:(j + 1) * rows], axis=1, keepdims=True)
    carry_ref[...] = carry
    w = _stick_weights(logb, sp.astype(BF16), wneg_ref[...], jnp.concatenate(carries, axis=0), None).astype(BF16)
    acc = acc_ref[...]
    for j, v_ref in enumerate(v_refs):
        acc = acc + jnp.einsum("hqk,hdk->hqd", w[j * rows:(j + 1) * rows].reshape(N_HEADS, SUBLANES, PAGE_SIZE),
                               v_ref[...].astype(BF16), preferred_element_type=F32)
    acc_ref[...] = acc

    @pl.when(step == pl.num_programs(1) - 1)
    def _():
        o_ref[...] = acc.astype(BF16)


def _attn_sample_call(page_table, q, cache_k, cache_v, page_base, bias_t, wneg):
    n_seq, n_pages = page_table.shape
    assert n_pages % PAGES_PER_STEP == 0

    def page(j):
        return lambda b, s, pt: (page_base + pt[b, n_pages - 1 - (s * PAGES_PER_STEP + j)], 0, 0, 0)

    page_specs = [pl.BlockSpec((None, N_HEADS, HEAD_DIM, PAGE_SIZE), page(j)) for j in range(PAGES_PER_STEP)]
    q_spec = pl.BlockSpec((None, N_HEADS, SUBLANES, HEAD_DIM), lambda b, s, pt: (b, 0, 0, 0))
    return pl.pallas_call(
        _attn_sample_kernel,
        grid_spec=pltpu.PrefetchScalarGridSpec(
            num_scalar_prefetch=1,
            grid=(n_seq, n_pages // PAGES_PER_STEP),
            in_specs=[q_spec,
                      pl.BlockSpec((N_HEADS, SUBLANES, PAGE_SIZE), lambda b, s, pt: (0, 0, 0)),
                      pl.BlockSpec((PAGE_SIZE, PAGE_SIZE), lambda b, s, pt: (0, 0))]
                     + page_specs + page_specs,
            out_specs=q_spec,
            scratch_shapes=[pltpu.VMEM((N_HEADS, SUBLANES, HEAD_DIM), F32),
                            pltpu.VMEM((N_HEADS * SUBLANES, 1), F32)]),
        out_shape=jax.ShapeDtypeStruct((n_seq, N_HEADS, SUBLANES, HEAD_DIM), BF16),
        compiler_params=_params(2),
        name="attn_sample",
    )(page_table, q, bias_t, wneg, *([cache_k] * PAGES_PER_STEP), *([cache_v] * PAGES_PER_STEP))


def _proj_res_kernel(h_ref, o_ref, w_ref, g_ref, out_ref):
    m = jnp.dot(o_ref[...], w_ref[...], preferred_element_type=F32)
    out_ref[...] = h_ref[...] + _rms(m, g_ref[...])


def _proj_res_call(h, o, w, g):
    n = h.shape[0]
    tm = min(ROW_TILE, n)
    row = pl.BlockSpec((tm, D_MODEL), lambda i: (i, 0))
    return pl.pallas_call(
        _proj_res_kernel,
        grid=(n // tm,),
        in_specs=[row, row, _full((D_MODEL, D_MODEL)), _full((1, D_MODEL))],
        out_specs=row,
        out_shape=jax.ShapeDtypeStruct((n, D_MODEL), F32),
        compiler_params=_params(1),
        name="attn_out_proj",
    )(h, o, w, g)


def _mlp_kernel(h_ref, g_pre_ref, g_post_ref, wu_ref, wd_ref, out_ref):
    x = h_ref[...]
    u = _rms(x, g_pre_ref[...]).astype(BF16)
    f = jnp.zeros(x.shape, F32)
    for c in range(D_FF // FF_CHUNK):
        sl = slice(c * FF_CHUNK, (c + 1) * FF_CHUNK)
        a = jnp.maximum(jnp.dot(u, wu_ref[:, sl], preferred_element_type=F32), 0.0)
        f = f + jnp.dot((a * a).astype(BF16), wd_ref[sl, :], preferred_element_type=F32)
    out_ref[...] = x + _rms(f, g_post_ref[...])


def _mlp_call(h, g_pre, g_post, wu, wd):
    n = h.shape[0]
    tm = min(ROW_TILE, n)
    row = pl.BlockSpec((tm, D_MODEL), lambda i: (i, 0))
    return pl.pallas_call(
        _mlp_kernel,
        grid=(n // tm,),
        in_specs=[row, _full((1, D_MODEL)), _full((1, D_MODEL)),
                  pl.BlockSpec((D_MODEL, D_FF), lambda i: (0, 0), pipeline_mode=pl.Buffered(1)),
                  pl.BlockSpec((D_FF, D_MODEL), lambda i: (0, 0), pipeline_mode=pl.Buffered(1))],
        out_specs=row,
        out_shape=jax.ShapeDtypeStruct((n, D_MODEL), F32),
        compiler_params=_params(1),
        name="mlp",
    )(h, g_pre, g_post, wu, wd)


def _pool_mix(u, shifted, cnt_of, wp_ref, ps):
    groups = []
    for gi, w in enumerate(POOL_WINDOWS):
        sl = slice(gi * POOL_GROUP, (gi + 1) * POOL_GROUP)
        ug = u[:, sl]
        win = ug
        for j in range(1, w):
            win = win + shifted(j, sl)
        d = win / cnt_of(w) - ug
        groups.append(jnp.dot(d.astype(BF16), wp_ref[gi], preferred_element_type=F32))
    return jnp.concatenate(groups, axis=1) * ps


def _pool_prompt_kernel(tail_tile, tail_off, h_ref, g_pre_ref, g_post_ref, wp_ref, ps_ref,
                        out_ref, tail_ref, ext_ref):
    t = pl.program_id(1)
    tt = h_ref.shape[0]

    @pl.when(t == 0)
    def _():
        ext_ref[0:POOL_BUF + 1, :] = jnp.zeros((POOL_BUF + 1, D_MODEL), F32)

    x = h_ref[...]
    u = _rms(x, g_pre_ref[...])
    hist = POOL_BUF + 1
    ext_ref[hist:hist + tt, :] = u
    pos = t * tt + lax.broadcasted_iota(jnp.int32, (tt, 1), 0)
    m = _pool_mix(u, lambda j, sl: ext_ref[hist - j:hist - j + tt, sl],
                  lambda w: jnp.minimum(pos + 1, w).astype(F32), wp_ref, ps_ref[...])
    out_ref[...] = x + _rms(m, g_post_ref[...])
    ext_ref[0:hist, :] = u[tt - hist:, :]

    @pl.when(t == tail_tile)
    def _():
        tail_ref[...] = u[tail_off:tail_off + hist, :]


def _pool_prompt_call(h, g_pre, g_post, wp, ps, batch, t_pad, t_real):
    tt = TIME_TILE
    nt = t_pad // tt
    hist = POOL_BUF + 1
    tail_start = t_real - hist
    kern = functools.partial(_pool_prompt_kernel, tail_start // tt, tail_start % tt)
    return pl.pallas_call(
        kern,
        grid=(batch, nt),
        in_specs=[pl.BlockSpec((tt, D_MODEL), lambda b, t: (b * nt + t, 0)),
                  _full((1, D_MODEL)), _full((1, D_MODEL)),
                  _full((len(POOL_WINDOWS), POOL_GROUP, POOL_GROUP)), _full((1, D_MODEL))],
        out_specs=[pl.BlockSpec((tt, D_MODEL), lambda b, t: (b * nt + t, 0)),
                   pl.BlockSpec((None, hist, D_MODEL), lambda b, t: (b, 0, 0))],
        out_shape=[jax.ShapeDtypeStruct((batch * t_pad, D_MODEL), F32),
                   jax.ShapeDtypeStruct((batch, hist, D_MODEL), F32)],
        scratch_shapes=[pltpu.VMEM((hist + tt, D_MODEL), F32)],
        compiler_params=_params(2),
        name="pool_prompt",
    )(h, g_pre, g_post, wp, ps)


def _pool_sample_kernel(h_ref, g_pre_ref, g_post_ref, wp_ref, ps_ref, st_ref, out_ref, u_ref):
    x = h_ref[...]
    u = _rms(x, g_pre_ref[...])
    u_ref[...] = u
    m = _pool_mix(u, lambda j, sl: st_ref[POOL_BUF - j, :, sl], lambda w: float(w), wp_ref, ps_ref[...])
    out_ref[...] = x + _rms(m, g_post_ref[...])


def _pool_sample_call(h, g_pre, g_post, wp, ps, state_t):
    n = h.shape[0]
    return pl.pallas_call(
        _pool_sample_kernel,
        out_shape=[jax.ShapeDtypeStruct((n, D_MODEL), F32), jax.ShapeDtypeStruct((n, D_MODEL), F32)],
        compiler_params=pltpu.CompilerParams(vmem_limit_bytes=VMEM_LIMIT),
        name="pool_sample",
    )(h, g_pre, g_post, wp, ps, state_t)


def _gelu_tanh(x):
    return 0.5 * x * (1.0 + jnp.tanh(math.sqrt(2.0 / math.pi) * (x + 0.044715 * (x * x * x))))


def _rg_branches(x, g_pre, wgx_ref):
    u = _rms(x, g_pre).astype(BF16)
    gx = jnp.dot(u, wgx_ref[...], preferred_element_type=F32)
    return _gelu_tanh(gx[:, :D_MODEL]), gx[:, D_MODEL:]


def _rg_gates(xc, wa_ref, ba, wi_ref, bi, lam):
    xcb = xc.astype(BF16)
    ra, ri = [], []
    for hd in range(RG_HEADS):
        sl = slice(hd * RG_BLOCK, (hd + 1) * RG_BLOCK)
        ra.append(jnp.dot(xcb[:, sl], wa_ref[hd], preferred_element_type=F32))
        ri.append(jnp.dot(xcb[:, sl], wi_ref[hd], preferred_element_type=F32))
    r = jax.nn.sigmoid(jnp.concatenate(ra, axis=1) + ba)
    ig = jax.nn.sigmoid(jnp.concatenate(ri, axis=1) + bi)
    log_a = -RG_C * r * _softplus(-lam)
    a = jnp.exp(log_a)
    mult = jnp.sqrt(-jnp.tanh(log_a) * (a * a + 1.0))
    return a, mult, ig * xc


def _rg_prompt_kernel(tail_tile, tail_off, h_ref, g_pre_ref, g_post_ref, wgx_ref, cw_ref, cb_ref, wa_ref,
                      ba_ref, wi_ref, bi_ref, lam_ref, wo_ref, out_ref, ctail_ref, htail_ref,
                      xext_ref, hc_ref):
    t = pl.program_id(1)
    tt = h_ref.shape[0]
    hist = 8

    @pl.when(t == 0)
    def _():
        xext_ref[0:hist, :] = jnp.zeros((hist, D_MODEL), F32)
        hc_ref[...] = jnp.zeros_like(hc_ref)

    x = h_ref[...]
    gate, xb = _rg_branches(x, g_pre_ref[...], wgx_ref)
    xext_ref[hist:hist + tt, :] = xb
    xc = cb_ref[...]
    for tap in range(CONV_WIDTH):
        off = hist - (CONV_WIDTH - 1) + tap
        xc = xc + xext_ref[off:off + tt, :] * cw_ref[tap:tap + 1, :]
    a, mult, gx = _rg_gates(xc, wa_ref, ba_ref[...], wi_ref, bi_ref[...], lam_ref[...])
    row = lax.broadcasted_iota(jnp.int32, (tt, 1), 0)
    mult = jnp.where((row == 0) & (t == 0), 1.0, mult)
    b = mult * gx
    k = 1
    while k < tt:
        keep = row >= k
        a_prev = jnp.where(keep, pltpu.roll(a, k, 0), 1.0)
        b_prev = jnp.where(keep, pltpu.roll(b, k, 0), 0.0)
        b = b + a * b_prev
        a = a * a_prev
        k *= 2
    hs = b + a * hc_ref[...]
    hc_ref[...] = hs[tt - 1:tt, :]
    y = (hs * gate).astype(BF16)
    out_ref[...] = x + _rms(jnp.dot(y, wo_ref[...], preferred_element_type=F32), g_post_ref[...])
    xext_ref[0:hist, :] = xb[tt - hist:, :]

    @pl.when(t == tail_tile)
    def _():
        ctail_ref[...] = xb[tail_off:tail_off + hist, :]
        htail_ref[...] = hs[tail_off:tail_off + hist, :]


def _rg_weight_specs():
    return [_full((1, D_MODEL)), _full((1, D_MODEL)), _full((D_MODEL, 2 * D_MODEL)),
            _full((CONV_WIDTH, D_MODEL)), _full((1, D_MODEL)),
            _full((RG_HEADS, RG_BLOCK, RG_BLOCK)), _full((1, D_MODEL)),
            _full((RG_HEADS, RG_BLOCK, RG_BLOCK)), _full((1, D_MODEL)),
            _full((1, D_MODEL)), _full((D_MODEL, D_MODEL))]


def _rg_prompt_call(h, weights, batch, t_pad, t_real):
    tt = TIME_TILE
    nt = t_pad // tt
    hist = 8
    tail_start = t_real - hist
    kern = functools.partial(_rg_prompt_kernel, tail_start // tt, tail_start % tt)
    tail = pl.BlockSpec((None, hist, D_MODEL), lambda b, t: (b, 0, 0))
    return pl.pallas_call(
        kern,
        grid=(batch, nt),
        in_specs=[pl.BlockSpec((tt, D_MODEL), lambda b, t: (b * nt + t, 0))] + _rg_weight_specs(),
        out_specs=[pl.BlockSpec((tt, D_MODEL), lambda b, t: (b * nt + t, 0)), tail, tail],
        out_shape=[jax.ShapeDtypeStruct((batch * t_pad, D_MODEL), F32),
                   jax.ShapeDtypeStruct((batch, hist, D_MODEL), F32),
                   jax.ShapeDtypeStruct((batch, hist, D_MODEL), F32)],
        scratch_shapes=[pltpu.VMEM((hist + tt, D_MODEL), F32), pltpu.VMEM((1, D_MODEL), F32)],
        compiler_params=_params(2),
        name="rglru_prompt",
    )(h, *weights)


def _rg_sample_kernel(h_ref, g_pre_ref, g_post_ref, wgx_ref, cw_ref, cb_ref, wa_ref, ba_ref, wi_ref, bi_ref,
                      lam_ref, wo_ref, cs_ref, hp_ref, out_ref, xb_ref, hn_ref):
    x = h_ref[...]
    gate, xb = _rg_branches(x, g_pre_ref[...], wgx_ref)
    xb_ref[...] = xb
    xc = cb_ref[...]
    for tap in range(CONV_WIDTH - 1):
        xc = xc + cs_ref[tap] * cw_ref[tap:tap + 1, :]
    xc = xc + xb * cw_ref[CONV_WIDTH - 1:CONV_WIDTH, :]
    a, mult, gx = _rg_gates(xc, wa_ref, ba_ref[...], wi_ref, bi_ref[...], lam_ref[...])
    hn = a * hp_ref[...] + mult * gx
    hn_ref[...] = hn
    y = (hn * gate).astype(BF16)
    out_ref[...] = x + _rms(jnp.dot(y, wo_ref[...], preferred_element_type=F32), g_post_ref[...])


def _rg_sample_call(h, weights, conv_state_t, h_prev):
    n = h.shape[0]
    out = jax.ShapeDtypeStruct((n, D_MODEL), F32)
    return pl.pallas_call(
        _rg_sample_kernel,
        out_shape=[out, out, out],
        compiler_params=pltpu.CompilerParams(vmem_limit_bytes=VMEM_LIMIT),
        name="rglru_sample",
    )(h, *weights, conv_state_t, h_prev)


def kernel(x_prompt, x_sample, cache_k, cache_v, page_table, state_pool, state_conv, state_h, meta_tokens, norm_g, attn_w_qkv, attn_w_o, attn_bias, pool_w, pool_scale, rg_w_gate, rg_w_x, rg_conv_w, rg_conv_b, rg_w_a, rg_b_a, rg_w_i, rg_b_i, rg_lambda, rg_w_out, mlp_w_up, mlp_w_down):
    batch, seq, _ = x_prompt.shape
    n_seq = x_sample.shape[0]
    t_real = seq + N_META
    t_unit = math.lcm(Q_BLOCK, TIME_TILE)
    t_pad = -(-t_real // t_unit) * t_unit
    assert (batch * t_pad) % ROW_TILE == 0 and x_sample.shape[1] == 1
    n_phys = cache_k.shape[1]

    meta = jnp.broadcast_to(meta_tokens[None], (batch, N_META, D_MODEL))
    pad = jnp.zeros((batch, t_pad - t_real, D_MODEL), F32)
    h_p = jnp.concatenate([meta, x_prompt, pad], axis=1).reshape(batch * t_pad, D_MODEL)
    h_s = x_sample.reshape(n_seq, D_MODEL)
    cache_k2 = cache_k.transpose(0, 1, 3, 4, 2).reshape(-1, N_HEADS, HEAD_DIM, PAGE_SIZE)
    cache_v2 = cache_v.transpose(0, 1, 3, 4, 2).reshape(-1, N_HEADS, HEAD_DIM, PAGE_SIZE)
    wneg_p = _suffix_matrix(K_CHUNK)
    wneg_s = _suffix_matrix(PAGE_SIZE)
    row = lambda v: v.reshape(1, D_MODEL)

    k_p, v_p, k_s, v_s = [], [], [], []
    pool_p, pool_s, conv_p, conv_s, hl_p, hl_s = [], [], [], [], [], []
    for layer in range(DEPTH):
        kind, j = layer % N_MIXERS, layer // N_MIXERS
        g = norm_g[layer]
        if kind == 0:
            w_qkv = attn_w_qkv[j].astype(BF16)
            w_o = attn_w_o[j].astype(BF16)
            kv, qkvb = _qkv_call(h_p, row(g[0]), w_qkv)
            bias2 = attn_bias[j] * LOG2E
            o = _attn_prompt_call(qkvb, bias2, wneg_p, batch, t_pad)
            h_p = _proj_res_call(h_p, o, w_o, row(g[1]))
            kv3 = kv.reshape(batch, t_pad, 2, N_HEADS, HEAD_DIM)[:, :t_real]
            k_p.append(kv3[:, :, 0])
            v_p.append(kv3[:, :, 1])

            kv, qkvb = _qkv_call(h_s, row(g[0]), w_qkv)
            bias_t = jnp.broadcast_to(bias2[:, None, None], (N_HEADS, SUBLANES, PAGE_SIZE))
            q_s = jnp.broadcast_to(qkvb[:, :D_MODEL].reshape(n_seq, N_HEADS, 1, HEAD_DIM),
                                   (n_seq, N_HEADS, SUBLANES, HEAD_DIM))
            o = _attn_sample_call(page_table, q_s, cache_k2, cache_v2, j * n_phys, bias_t, wneg_s)
            h_s = _proj_res_call(h_s, o[:, :, 0].reshape(n_seq, D_MODEL), w_o, row(g[1]))
            kv3 = kv.reshape(n_seq, 1, 2, N_HEADS, HEAD_DIM)
            k_s.append(kv3[:, :, 0])
            v_s.append(kv3[:, :, 1])
        elif kind == 1:
            wp = pool_w[j].astype(BF16)
            ps = row(pool_scale[j])
            h_p, tail = _pool_prompt_call(h_p, row(g[0]), row(g[1]), wp, ps, batch, t_pad, t_real)
            pool_p.append(tail[:, 1:])
            st = state_pool[j]
            h_s, u_s = _pool_sample_call(h_s, row(g[0]), row(g[1]), wp, ps, jnp.swapaxes(st, 0, 1))
            pool_s.append(jnp.concatenate([st[:, 1:], u_s[:, None]], axis=1))
        else:
            weights = (row(g[0]), row(g[1]),
                       jnp.concatenate([rg_w_gate[j], rg_w_x[j]], axis=1).astype(BF16),
                       rg_conv_w[j], row(rg_conv_b[j]),
                       rg_w_a[j].astype(BF16), row(rg_b_a[j]), rg_w_i[j].astype(BF16), row(rg_b_i[j]),
                       row(rg_lambda[j]), rg_w_out[j].astype(BF16))
            h_p, ctail, htail = _rg_prompt_call(h_p, weights, batch, t_pad, t_real)
            conv_p.append(ctail[:, 8 - (CONV_WIDTH - 1):])
            hl_p.append(htail[:, 7])
            cs = state_conv[j]
            h_s, xb_s, hn_s = _rg_sample_call(h_s, weights, jnp.swapaxes(cs, 0, 1), state_h[j])
            conv_s.append(jnp.concatenate([cs[:, 1:], xb_s[:, None]], axis=1))
            hl_s.append(hn_s)
        w_up = mlp_w_up[layer].astype(BF16)
        w_down = mlp_w_down[layer].astype(BF16)
        h_p = _mlp_call(h_p, row(g[2]), row(g[3]), w_up, w_down)
        h_s = _mlp_call(h_s, row(g[2]), row(g[3]), w_up, w_down)

    y_prompt = h_p.reshape(batch, t_pad, D_MODEL)[:, N_META:t_real]
    y_sample = h_s.reshape(n_seq, 1, D_MODEL)
    return (y_prompt, y_sample, jnp.stack(k_p), jnp.stack(v_p), jnp.stack(k_s), jnp.stack(v_s),
            jnp.stack(pool_p), jnp.stack(pool_s), jnp.stack(conv_p), jnp.stack(conv_s),
            jnp.stack(hl_p), jnp.stack(hl_s))
```

```python
import functools
import math

import jax
import jax.numpy as jnp
from jax import lax
from jax.experimental import pallas as pl
from jax.experimental.pallas import tpu as pltpu

F32 = jnp.float32
BF16 = jnp.bfloat16

D_MODEL = 1024
N_HEADS = 16
HEAD_DIM = D_MODEL // N_HEADS
N_PAIRS = N_HEADS // 2
D_FF = 4 * D_MODEL
N_META = 16
N_MIXERS = 3
DEPTH = 4
PAGE_SIZE = 128
POOL_WINDOWS = (2, 4, 8, 16)
POOL_GROUP = D_MODEL // len(POOL_WINDOWS)
POOL_BUF = max(POOL_WINDOWS) - 1
RG_HEADS = 4
RG_BLOCK = D_MODEL // RG_HEADS
CONV_WIDTH = 4
RG_C = 8.0
RMS_EPS = 1e-6
LOG2E = math.log2(math.e)

LANES = 128
SUBLANES = 8
Q_BLOCK = 256
K_CHUNK = 256
PAGES_PER_STEP = 16
ROW_TILE = 512
TIME_TILE = 256
FF_CHUNK = 1024
VMEM_LIMIT = 56 * 1024 * 1024


def _params(n_axes, vmem=VMEM_LIMIT):
    return pltpu.CompilerParams(dimension_semantics=("arbitrary",) * n_axes, vmem_limit_bytes=vmem)


def _rms(x, g):
    ms = jnp.mean(x * x, axis=-1, keepdims=True)
    return x * lax.rsqrt(ms + RMS_EPS) * g


def _softplus(x):
    return jnp.maximum(x, 0.0) + jnp.log1p(jnp.exp(-jnp.abs(x)))


def _full(shape):
    nd = len(shape)
    return pl.BlockSpec(shape, lambda *_: (0,) * nd)


def _qkv_kernel(x_ref, g_ref, w_ref, kv_ref, qkvb_ref):
    u = _rms(x_ref[...], g_ref[...]).astype(BF16)
    y = jnp.dot(u, w_ref[...], preferred_element_type=F32)
    kv_ref[...] = y[:, D_MODEL:]
    qkvb_ref[:, :D_MODEL] = (y[:, :D_MODEL] * (LOG2E * HEAD_DIM ** -0.5)).astype(BF16)
    qkvb_ref[:, D_MODEL:2 * D_MODEL] = y[:, D_MODEL:2 * D_MODEL].astype(BF16)
    v = y[:, 2 * D_MODEL:]
    even = (lax.broadcasted_iota(jnp.int32, v.shape, 1) & HEAD_DIM) == 0
    qkvb_ref[:, 2 * D_MODEL:3 * D_MODEL] = jnp.where(even, v, 0.0).astype(BF16)
    qkvb_ref[:, 3 * D_MODEL:] = jnp.where(even, 0.0, v).astype(BF16)


def _qkv_call(h, g, w):
    n = h.shape[0]
    tm = min(ROW_TILE, n)
    return pl.pallas_call(
        _qkv_kernel,
        grid=(n // tm,),
        in_specs=[pl.BlockSpec((tm, D_MODEL), lambda i: (i, 0)), _full((1, D_MODEL)),
                  _full((D_MODEL, 3 * D_MODEL))],
        out_specs=[pl.BlockSpec((tm, 2 * D_MODEL), lambda i: (i, 0)),
                   pl.BlockSpec((tm, 4 * D_MODEL), lambda i: (i, 0))],
        out_shape=[jax.ShapeDtypeStruct((n, 2 * D_MODEL), F32),
                   jax.ShapeDtypeStruct((n, 4 * D_MODEL), BF16)],
        compiler_params=_params(1),
        name="qkv_proj",
    )(h, g, w)


def _suffix_matrix(n):
    j = jnp.arange(n)[:, None]
    s = jnp.arange(n)[None, :]
    return jnp.where(j > s, -1.0, 0.0).astype(BF16)


def _stick_prep(z2, mask):
    sp = jnp.maximum(z2, 0.0) + jnp.log(1.0 + jnp.exp2(-jnp.abs(z2))) * LOG2E
    logb = z2 - sp
    return logb, sp if mask is None else jnp.where(mask, sp, 0.0)


def _stick_weights(logb, sp, wneg, carry, mask):
    w = jnp.exp2(logb + jnp.dot(sp, wneg, preferred_element_type=F32) + carry)
    return w if mask is None else jnp.where(mask, w, 0.0)


def _attn_prompt_kernel(bias_ref, q_ref, k_ref, v0_ref, v1_ref, wneg_ref, ones2_ref, o_ref,
                        q2_ref, logb_ref, sp_ref, tot_ref, w_ref, carry_ref, acc_ref):
    p = pl.program_id(1)
    m = 2 * Q_BLOCK
    lane = lax.broadcasted_iota(jnp.int32, (Q_BLOCK, LANES), 1)
    row = lax.broadcasted_iota(jnp.int32, (m, K_CHUNK), 0)
    col = lax.broadcasted_iota(jnp.int32, (m, K_CHUNK), 1)
    diag_mask = col < (row & (Q_BLOCK - 1))
    wneg = wneg_ref[...]
    ones2 = ones2_ref[...]
    row_m = lax.broadcasted_iota(jnp.int32, (m, LANES), 0)
    lane_m = lax.broadcasted_iota(jnp.int32, (m, LANES), 1)
    bias = jnp.where(row_m < Q_BLOCK, bias_ref[2 * p], bias_ref[2 * p + 1])
    bias_hi = bias.astype(BF16).astype(F32)
    q2_ref[:, LANES:] = jnp.where(lane_m == 0, bias_hi, jnp.where(lane_m == 1, bias - bias_hi, 0.0)).astype(BF16)

    def chunk(ref, kb):
        return ref[pl.ds(pl.multiple_of(jnp.maximum(kb, 0) * K_CHUNK, K_CHUNK), K_CHUNK), :]

    def logits(kb, mask):
        k2 = jnp.concatenate([chunk(k_ref, kb), ones2], axis=1)
        z = lax.dot_general(q2_ref[...], k2, (((1,), (1,)), ((), ())), preferred_element_type=F32)
        logb_ref[...], sp = _stick_prep(z, mask)
        sp_ref[...] = sp.astype(BF16)
        tot_ref[...] = sp[:, :LANES] + sp[:, LANES:]

    def weights(mask):
        carry = carry_ref[...]
        w_ref[...] = _stick_weights(logb_ref[...], sp_ref[...], wneg, carry, mask).astype(BF16)
        carry_ref[...] = carry - jnp.sum(tot_ref[...], axis=1, keepdims=True)

    def weighted_values(kb):
        w2 = jnp.concatenate([w_ref[:Q_BLOCK, :], w_ref[Q_BLOCK:, :]], axis=1)
        v2 = jnp.concatenate([chunk(v0_ref, kb), chunk(v1_ref, kb)], axis=0)
        acc_ref[...] += jnp.dot(w2, v2, preferred_element_type=F32)

    def trip(kb):
        weighted_values(kb)
        weights(None)
        logits(kb - 2, None)

    def query_block(qb, _):
        q = chunk(q_ref, qb).astype(F32)
        q2_ref[:, :LANES] = jnp.concatenate(
            [jnp.where(lane < HEAD_DIM, q, 0.0), jnp.where(lane >= HEAD_DIM, q, 0.0)], axis=0).astype(BF16)
        carry_ref[...] = jnp.zeros_like(carry_ref)
        acc_ref[...] = jnp.zeros_like(acc_ref)
        logits(qb, diag_mask)
        weights(diag_mask)
        logits(qb - 1, None)

        def two_trips(i, _):
            trip(qb - 2 * i)
            trip(qb - 2 * i - 1)
            return 0

        n_trips = qb + 1
        lax.fori_loop(0, n_trips // 2, two_trips, 0)

        @pl.when(n_trips % 2 == 1)
        def _():
            trip(0)

        o_ref[pl.ds(pl.multiple_of(qb * Q_BLOCK, Q_BLOCK), Q_BLOCK), :] = acc_ref[...].astype(BF16)
        return 0

    lax.fori_loop(0, q_ref.shape[0] // Q_BLOCK, query_block, 0)


def _attn_prompt_call(qkvb, bias, wneg, batch, t_pad):
    assert Q_BLOCK == K_CHUNK == 2 * LANES and t_pad % Q_BLOCK == 0
    m = 2 * Q_BLOCK

    def cols(section):
        return pl.BlockSpec((t_pad, LANES), lambda b, p: (b, section * N_PAIRS + p))

    ones2 = jnp.broadcast_to((jnp.arange(LANES) < 2).astype(BF16), (K_CHUNK, LANES))
    return pl.pallas_call(
        _attn_prompt_kernel,
        grid=(batch, N_PAIRS),
        in_specs=[pl.BlockSpec(memory_space=pltpu.SMEM), cols(0), cols(1), cols(2), cols(3),
                  _full((K_CHUNK, K_CHUNK)), _full((K_CHUNK, LANES))],
        out_specs=cols(0),
        out_shape=jax.ShapeDtypeStruct((batch * t_pad, D_MODEL), BF16),
        scratch_shapes=[pltpu.VMEM((m, 2 * LANES), BF16),
                        pltpu.VMEM((m, K_CHUNK), F32), pltpu.VMEM((m, K_CHUNK), BF16), pltpu.VMEM((m, LANES), F32),
                        pltpu.VMEM((m, K_CHUNK), BF16), pltpu.VMEM((m, 1), F32),
                        pltpu.VMEM((Q_BLOCK, LANES), F32)],
        compiler_params=_params(2),
        name="attn_prompt",
    )(bias, qkvb, qkvb, qkvb, qkvb, wneg, ones2)


def _attn_sample_kernel(pt_ref, q_ref, bias_ref, wneg_ref, *refs):
    k_refs = refs[:PAGES_PER_STEP]
    v_refs = refs[PAGES_PER_STEP:2 * PAGES_PER_STEP]
    o_ref, acc_ref, carry_ref = refs[2 * PAGES_PER_STEP:]
    step = pl.program_id(1)

    @pl.when(step == 0)
    def _():
        acc_ref[...] = jnp.zeros_like(acc_ref)
        carry_ref[...] = jnp.zeros_like(carry_ref)

    q = q_ref[...]
    rows = N_HEADS * SUBLANES
    z = jnp.concatenate(
        [(jnp.einsum("hqd,hdk->hqk", q, k_ref[...].astype(BF16), preferred_element_type=F32)
          + bias_ref[...]).reshape(rows, PAGE_SIZE) for k_ref in k_refs], axis=0)
    logb, sp = _stick_prep(z, None)
    carry = carry_ref[...]
    carries = []
    for j in range(PAGES_PER_STEP):
        carries.append(carry)
        carry = carry - jnp.sum(sp[j * rows:(j + 1) * rows], axis=1, keepdims=True)
    carry_ref[...] = carry
    w = _stick_weights(logb, sp.astype(BF16), wneg_ref[...], jnp.concatenate(carries, axis=0), None).astype(BF16)
    acc = acc_ref[...]
    for j, v_ref in enumerate(v_refs):
        acc = acc + jnp.einsum("hqk,hdk->hqd", w[j * rows:(j + 1) * rows].reshape(N_HEADS, SUBLANES, PAGE_SIZE),
                               v_ref[...].astype(BF16), preferred_element_type=F32)
    acc_ref[...] = acc

    @pl.when(step == pl.num_programs(1) - 1)
    def _():
        o_ref[...] = acc.astype(BF16)


def _attn_sample_call(page_table, q, cache_k, cache_v, page_base, bias_t, wneg):
    n_seq, n_pages = page_table.shape
    assert n_pages % PAGES_PER_STEP == 0

    def page(j):
        return lambda b, s, pt: (page_base + pt[b, n_pages - 1 - (s * PAGES_PER_STEP + j)], 0, 0, 0)

    page_specs = [pl.BlockSpec((None, N_HEADS, HEAD_DIM, PAGE_SIZE), page(j)) for j in range(PAGES_PER_STEP)]
    q_spec = pl.BlockSpec((None, N_HEADS, SUBLANES, HEAD_DIM), lambda b, s, pt: (b, 0, 0, 0))
    return pl.pallas_call(
        _attn_sample_kernel,
        grid_spec=pltpu.PrefetchScalarGridSpec(
            num_scalar_prefetch=1,
            grid=(n_seq, n_pages // PAGES_PER_STEP),
            in_specs=[q_spec,
                      pl.BlockSpec((N_HEADS, SUBLANES, PAGE_SIZE), lambda b, s, pt: (0, 0, 0)),
                      pl.BlockSpec((PAGE_SIZE, PAGE_SIZE), lambda b, s, pt: (0, 0))]
                     + page_specs + page_specs,
            out_specs=q_spec,
            scratch_shapes=[pltpu.VMEM((N_HEADS, SUBLANES, HEAD_DIM), F32),
                            pltpu.VMEM((N_HEADS * SUBLANES, 1), F32)]),
        out_shape=jax.ShapeDtypeStruct((n_seq, N_HEADS, SUBLANES, HEAD_DIM), BF16),
        compiler_params=_params(2),
        name="attn_sample",
    )(page_table, q, bias_t, wneg, *([cache_k] * PAGES_PER_STEP), *([cache_v] * PAGES_PER_STEP))


def _proj_res_kernel(h_ref, o_ref, w_ref, g_ref, out_ref):
    m = jnp.dot(o_ref[...], w_ref[...], preferred_element_type=F32)
    out_ref[...] = h_ref[...] + _rms(m, g_ref[...])


def _proj_res_call(h, o, w, g):
    n = h.shape[0]
    tm = min(ROW_TILE, n)
    row = pl.BlockSpec((tm, D_MODEL), lambda i: (i, 0))
    return pl.pallas_call(
        _proj_res_kernel,
        grid=(n // tm,),
        in_specs=[row, row, _full((D_MODEL, D_MODEL)), _full((1, D_MODEL))],
        out_specs=row,
        out_shape=jax.ShapeDtypeStruct((n, D_MODEL), F32),
        compiler_params=_params(1),
        name="attn_out_proj",
    )(h, o, w, g)


def _mlp_kernel(h_ref, g_pre_ref, g_post_ref, wu_ref, wd_ref, out_ref):
    x = h_ref[...]
    u = _rms(x, g_pre_ref[...]).astype(BF16)
    f = jnp.zeros(x.shape, F32)
    for c in range(D_FF // FF_CHUNK):
        sl = slice(c * FF_CHUNK, (c + 1) * FF_CHUNK)
        a = jnp.maximum(jnp.dot(u, wu_ref[:, sl], preferred_element_type=F32), 0.0)
        f = f + jnp.dot((a * a).astype(BF16), wd_ref[sl, :], preferred_element_type=F32)
    out_ref[...] = x + _rms(f, g_post_ref[...])


def _mlp_call(h, g_pre, g_post, wu, wd):
    n = h.shape[0]
    tm = min(ROW_TILE, n)
    row = pl.BlockSpec((tm, D_MODEL), lambda i: (i, 0))
    return pl.pallas_call(
        _mlp_kernel,
        grid=(n // tm,),
        in_specs=[row, _full((1, D_MODEL)), _full((1, D_MODEL)),
                  pl.BlockSpec((D_MODEL, D_FF), lambda i: (0, 0), pipeline_mode=pl.Buffered(1)),
                  pl.BlockSpec((D_FF, D_MODEL), lambda i: (0, 0), pipeline_mode=pl.Buffered(1))],
        out_specs=row,
        out_shape=jax.ShapeDtypeStruct((n, D_MODEL), F32),
        compiler_params=_params(1),
        name="mlp",
    )(h, g_pre, g_post, wu, wd)


def _pool_mix(u, shifted, cnt_of, wp_ref, ps):
    groups = []
    for gi, w in enumerate(POOL_WINDOWS):
        sl = slice(gi * POOL_GROUP, (gi + 1) * POOL_GROUP)
        ug = u[:, sl]
        win = ug
        for j in range(1, w):
            win = win + shifted(j, sl)
        d = win / cnt_of(w) - ug
        groups.append(jnp.dot(d.astype(BF16), wp_ref[gi], preferred_element_type=F32))
    return jnp.concatenate(groups, axis=1) * ps


def _pool_prompt_kernel(tail_tile, tail_off, h_ref, g_pre_ref, g_post_ref, wp_ref, ps_ref,
                        out_ref, tail_ref, ext_ref):
    t = pl.program_id(1)
    tt = h_ref.shape[0]

    @pl.when(t == 0)
    def _():
        ext_ref[0:POOL_BUF + 1, :] = jnp.zeros((POOL_BUF + 1, D_MODEL), F32)

    x = h_ref[...]
    u = _rms(x, g_pre_ref[...])
    hist = POOL_BUF + 1
    ext_ref[hist:hist + tt, :] = u
    pos = t * tt + lax.broadcasted_iota(jnp.int32, (tt, 1), 0)
    m = _pool_mix(u, lambda j, sl: ext_ref[hist - j:hist - j + tt, sl],
                  lambda w: jnp.minimum(pos + 1, w).astype(F32), wp_ref, ps_ref[...])
    out_ref[...] = x + _rms(m, g_post_ref[...])
    ext_ref[0:hist, :] = u[tt - hist:, :]

    @pl.when(t == tail_tile)
    def _():
        tail_ref[...] = u[tail_off:tail_off + hist, :]


def _pool_prompt_call(h, g_pre, g_post, wp, ps, batch, t_pad, t_real):
    tt = TIME_TILE
    nt = t_pad // tt
    hist = POOL_BUF + 1
    tail_start = t_real - hist
    kern = functools.partial(_pool_prompt_kernel, tail_start // tt, tail_start % tt)
    return pl.pallas_call(
        kern,
        grid=(batch, nt),
        in_specs=[pl.BlockSpec((tt, D_MODEL), lambda b, t: (b * nt + t, 0)),
                  _full((1, D_MODEL)), _full((1, D_MODEL)),
                  _full((len(POOL_WINDOWS), POOL_GROUP, POOL_GROUP)), _full((1, D_MODEL))],
        out_specs=[pl.BlockSpec((tt, D_MODEL), lambda b, t: (b * nt + t, 0)),
                   pl.BlockSpec((None, hist, D_MODEL), lambda b, t: (b, 0, 0))],
        out_shape=[jax.ShapeDtypeStruct((batch * t_pad, D_MODEL), F32),
                   jax.ShapeDtypeStruct((batch, hist, D_MODEL), F32)],
        scratch_shapes=[pltpu.VMEM((hist + tt, D_MODEL), F32)],
        compiler_params=_params(2),
        name="pool_prompt",
    )(h, g_pre, g_post, wp, ps)


def _pool_sample_kernel(h_ref, g_pre_ref, g_post_ref, wp_ref, ps_ref, st_ref, out_ref, u_ref):
    x = h_ref[...]
    u = _rms(x, g_pre_ref[...])
    u_ref[...] = u
    m = _pool_mix(u, lambda j, sl: st_ref[POOL_BUF - j, :, sl], lambda w: float(w), wp_ref, ps_ref[...])
    out_ref[...] = x + _rms(m, g_post_ref[...])


def _pool_sample_call(h, g_pre, g_post, wp, ps, state_t):
    n = h.shape[0]
    return pl.pallas_call(
        _pool_sample_kernel,
        out_shape=[jax.ShapeDtypeStruct((n, D_MODEL), F32), jax.ShapeDtypeStruct((n, D_MODEL), F32)],
        compiler_params=pltpu.CompilerParams(vmem_limit_bytes=VMEM_LIMIT),
        name="pool_sample",
    )(h, g_pre, g_post, wp, ps, state_t)


def _gelu_tanh(x):
    return 0.5 * x * (1.0 + jnp.tanh(math.sqrt(2.0 / math.pi) * (x + 0.044715 * (x * x * x))))


def _rg_branches(x, g_pre, wgx_ref):
    u = _rms(x, g_pre).astype(BF16)
    gx = jnp.dot(u, wgx_ref[...], preferred_element_type=F32)
    return _gelu_tanh(gx[:, :D_MODEL]), gx[:, D_MODEL:]


def _rg_gates(xc, wa_ref, ba, wi_ref, bi, lam):
    xcb = xc.astype(BF16)
    ra, ri = [], []
    for hd in range(RG_HEADS):
        sl = slice(hd * RG_BLOCK, (hd + 1) * RG_BLOCK)
        ra.append(jnp.dot(xcb[:, sl], wa_ref[hd], preferred_element_type=F32))
        ri.append(jnp.dot(xcb[:, sl], wi_ref[hd], preferred_element_type=F32))
    r = jax.nn.sigmoid(jnp.concatenate(ra, axis=1) + ba)
    ig = jax.nn.sigmoid(jnp.concatenate(ri, axis=1) + bi)
    log_a = -RG_C * r * _softplus(-lam)
    a = jnp.exp(log_a)
    mult = jnp.sqrt(-jnp.tanh(log_a) * (a * a + 1.0))
    return a, mult, ig * xc


def _rg_prompt_kernel(tail_tile, tail_off, h_ref, g_pre_ref, g_post_ref, wgx_ref, cw_ref, cb_ref, wa_ref,
                      ba_ref, wi_ref, bi_ref, lam_ref, wo_ref, out_ref, ctail_ref, htail_ref,
                      xext_ref, hc_ref):
    t = pl.program_id(1)
    tt = h_ref.shape[0]
    hist = 8

    @pl.when(t == 0)
    def _():
        xext_ref[0:hist, :] = jnp.zeros((hist, D_MODEL), F32)
        hc_ref[...] = jnp.zeros_like(hc_ref)

    x = h_ref[...]
    gate, xb = _rg_branches(x, g_pre_ref[...], wgx_ref)
    xext_ref[hist:hist + tt, :] = xb
    xc = cb_ref[...]
    for tap in range(CONV_WIDTH):
        off = hist - (CONV_WIDTH - 1) + tap
        xc = xc + xext_ref[off:off + tt, :] * cw_ref[tap:tap + 1, :]
    a, mult, gx = _rg_gates(xc, wa_ref, ba_ref[...], wi_ref, bi_ref[...], lam_ref[...])
    row = lax.broadcasted_iota(jnp.int32, (tt, 1), 0)
    mult = jnp.where((row == 0) & (t == 0), 1.0, mult)
    b = mult * gx
    k = 1
    while k < tt:
        keep = row >= k
        a_prev = jnp.where(keep, pltpu.roll(a, k, 0), 1.0)
        b_prev = jnp.where(keep, pltpu.roll(b, k, 0), 0.0)
        b = b + a * b_prev
        a = a * a_prev
        k *= 2
    hs = b + a * hc_ref[...]
    hc_ref[...] = hs[tt - 1:tt, :]
    y = (hs * gate).astype(BF16)
    out_ref[...] = x + _rms(jnp.dot(y, wo_ref[...], preferred_element_type=F32), g_post_ref[...])
    xext_ref[0:hist, :] = xb[tt - hist:, :]

    @pl.when(t == tail_tile)
    def _():
        ctail_ref[...] = xb[tail_off:tail_off + hist, :]
        htail_ref[...] = hs[tail_off:tail_off + hist, :]


def _rg_weight_specs():
    return [_full((1, D_MODEL)), _full((1, D_MODEL)), _full((D_MODEL, 2 * D_MODEL)),
            _full((CONV_WIDTH, D_MODEL)), _full((1, D_MODEL)),
            _full((RG_HEADS, RG_BLOCK, RG_BLOCK)), _full((1, D_MODEL)),
            _full((RG_HEADS, RG_BLOCK, RG_BLOCK)), _full((1, D_MODEL)),
            _full((1, D_MODEL)), _full((D_MODEL, D_MODEL))]


def _rg_prompt_call(h, weights, batch, t_pad, t_real):
    tt = TIME_TILE
    nt = t_pad // tt
    hist = 8
    tail_start = t_real - hist
    kern = functools.partial(_rg_prompt_kernel, tail_start // tt, tail_start % tt)
    tail = pl.BlockSpec((None, hist, D_MODEL), lambda b, t: (b, 0, 0))
    return pl.pallas_call(
        kern,
        grid=(batch, nt),
        in_specs=[pl.BlockSpec((tt, D_MODEL), lambda b, t: (b * nt + t, 0))] + _rg_weight_specs(),
        out_specs=[pl.BlockSpec((tt, D_MODEL), lambda b, t: (b * nt + t, 0)), tail, tail],
        out_shape=[jax.ShapeDtypeStruct((batch * t_pad, D_MODEL), F32),
                   jax.ShapeDtypeStruct((batch, hist, D_MODEL), F32),
                   jax.ShapeDtypeStruct((batch, hist, D_MODEL), F32)],
        scratch_shapes=[pltpu.VMEM((hist + tt, D_MODEL), F32), pltpu.VMEM((1, D_MODEL), F32)],
        compiler_params=_params(2),
        name="rglru_prompt",
    )(h, *weights)


def _rg_sample_kernel(h_ref, g_pre_ref, g_post_ref, wgx_ref, cw_ref, cb_ref, wa_ref, ba_ref, wi_ref, bi_ref,
                      lam_ref, wo_ref, cs_ref, hp_ref, out_ref, xb_ref, hn_ref):
    x = h_ref[...]
    gate, xb = _rg_branches(x, g_pre_ref[...], wgx_ref)
    xb_ref[...] = xb
    xc = cb_ref[...]
    for tap in range(CONV_WIDTH - 1):
        xc = xc + cs_ref[tap] * cw_ref[tap:tap + 1, :]
    xc = xc + xb * cw_ref[CONV_WIDTH - 1:CONV_WIDTH, :]
    a, mult, gx = _rg_gates(xc, wa_ref, ba_ref[...], wi_ref, bi_ref[...], lam_ref[...])
    hn = a * hp_ref[...] + mult * gx
    hn_ref[...] = hn
    y = (hn * gate).astype(BF16)
    out_ref[...] = x + _rms(jnp.dot(y, wo_ref[...], preferred_element_type=F32), g_post_ref[...])


def _rg_sample_call(h, weights, conv_state_t, h_prev):
    n = h.shape[0]
    out = jax.ShapeDtypeStruct((n, D_MODEL), F32)
    return pl.pallas_call(
        _rg_sample_kernel,
        out_shape=[out, out, out],
        compiler_params=pltpu.CompilerParams(vmem_limit_bytes=VMEM_LIMIT),
        name="rglru_sample",
    )(h, *weights, conv_state_t, h_prev)


def kernel(x_prompt, x_sample, cache_k, cache_v, page_table, state_pool, state_conv, state_h, meta_tokens, norm_g, attn_w_qkv, attn_w_o, attn_bias, pool_w, pool_scale, rg_w_gate, rg_w_x, rg_conv_w, rg_conv_b, rg_w_a, rg_b_a, rg_w_i, rg_b_i, rg_lambda, rg_w_out, mlp_w_up, mlp_w_down):
    batch, seq, _ = x_prompt.shape
    n_seq = x_sample.shape[0]
    t_real = seq + N_META
    t_unit = math.lcm(Q_BLOCK, TIME_TILE)
    t_pad = -(-t_real // t_unit) * t_unit
    assert (batch * t_pad) % ROW_TILE == 0 and x_sample.shape[1] == 1
    n_phys = cache_k.shape[1]

    meta = jnp.broadcast_to(meta_tokens[None], (batch, N_META, D_MODEL))
    pad = jnp.zeros((batch, t_pad - t_real, D_MODEL), F32)
    h_p = jnp.concatenate([meta, x_prompt, pad], axis=1).reshape(batch * t_pad, D_MODEL)
    h_s = x_sample.reshape(n_seq, D_MODEL)
    cache_k2 = cache_k.transpose(0, 1, 3, 4, 2).reshape(-1, N_HEADS, HEAD_DIM, PAGE_SIZE)
    cache_v2 = cache_v.transpose(0, 1, 3, 4, 2).reshape(-1, N_HEADS, HEAD_DIM, PAGE_SIZE)
    wneg_p = _suffix_matrix(K_CHUNK)
    wneg_s = _suffix_matrix(PAGE_SIZE)
    row = lambda v: v.reshape(1, D_MODEL)

    k_p, v_p, k_s, v_s = [], [], [], []
    pool_p, pool_s, conv_p, conv_s, hl_p, hl_s = [], [], [], [], [], []
    for layer in range(DEPTH):
        kind, j = layer % N_MIXERS, layer // N_MIXERS
        g = norm_g[layer]
        if kind == 0:
            w_qkv = attn_w_qkv[j].astype(BF16)
            w_o = attn_w_o[j].astype(BF16)
            kv, qkvb = _qkv_call(h_p, row(g[0]), w_qkv)
            bias2 = attn_bias[j] * LOG2E
            o = _attn_prompt_call(qkvb, bias2, wneg_p, batch, t_pad)
            h_p = _proj_res_call(h_p, o, w_o, row(g[1]))
            kv3 = kv.reshape(batch, t_pad, 2, N_HEADS, HEAD_DIM)[:, :t_real]
            k_p.append(kv3[:, :, 0])
            v_p.append(kv3[:, :, 1])

            kv, qkvb = _qkv_call(h_s, row(g[0]), w_qkv)
            bias_t = jnp.broadcast_to(bias2[:, None, None], (N_HEADS, SUBLANES, PAGE_SIZE))
            q_s = jnp.broadcast_to(qkvb[:, :D_MODEL].reshape(n_seq, N_HEADS, 1, HEAD_DIM),
                                   (n_seq, N_HEADS, SUBLANES, HEAD_DIM))
            o = _attn_sample_call(page_table, q_s, cache_k2, cache_v2, j * n_phys, bias_t, wneg_s)
            h_s = _proj_res_call(h_s, o[:, :, 0].reshape(n_seq, D_MODEL), w_o, row(g[1]))
            kv3 = kv.reshape(n_seq, 1, 2, N_HEADS, HEAD_DIM)
            k_s.append(kv3[:, :, 0])
            v_s.append(kv3[:, :, 1])
        elif kind == 1:
            wp = pool_w[j].astype(BF16)
            ps = row(pool_scale[j])
            h_p, tail = _pool_prompt_call(h_p, row(g[0]), row(g[1]), wp, ps, batch, t_pad, t_real)
            pool_p.append(tail[:, 1:])
            st = state_pool[j]
            h_s, u_s = _pool_sample_call(h_s, row(g[0]), row(g[1]), wp, ps, jnp.swapaxes(st, 0, 1))
            pool_s.append(jnp.concatenate([st[:, 1:], u_s[:, None]], axis=1))
        else:
            weights = (row(g[0]), row(g[1]),
                       jnp.concatenate([rg_w_gate[j], rg_w_x[j]], axis=1).astype(BF16),
                       rg_conv_w[j], row(rg_conv_b[j]),
                       rg_w_a[j].astype(BF16), row(rg_b_a[j]), rg_w_i[j].astype(BF16), row(rg_b_i[j]),
                       row(rg_lambda[j]), rg_w_out[j].astype(BF16))
            h_p, ctail, htail = _rg_prompt_call(h_p, weights, batch, t_pad, t_real)
            conv_p.append(ctail[:, 8 - (CONV_WIDTH - 1):])
            hl_p.append(htail[:, 7])
            cs = state_conv[j]
            h_s, xb_s, hn_s = _rg_sample_call(h_s, weights, jnp.swapaxes(cs, 0, 1), state_h[j])
            conv_s.append(jnp.concatenate([cs[:, 1:], xb_s[:, None]], axis=1))
            hl_s.append(hn_s)
        w_up = mlp_w_up[layer].astype(BF16)
        w_down = mlp_w_down[layer].astype(BF16)
        h_p = _mlp_call(h_p, row(g[2]), row(g[3]), w_up, w_down)
        h_s = _mlp_call(h_s, row(g[2]), row(g[3]), w_up, w_down)

    y_prompt = h_p.reshape(batch, t_pad, D_MODEL)[:, N_META:t_real]
    y_sample = h_s.reshape(n_seq, 1, D_MODEL)
    return (y_prompt, y_sample, jnp.stack(k_p), jnp.stack(v_p), jnp.stack(k_s), jnp.stack(v_s),
            jnp.stack(pool_p), jnp.stack(pool_s), jnp.stack(conv_p), jnp.stack(conv_s),
            jnp.stack(hl_p), jnp.stack(hl_s))
```
